```python
import math
import jax
import jax.numpy as jnp
from jax import lax
import numpy as np

D_MODEL = 1024
BATCH = 16
SEQ = 2048
DEPTH = 2

MEM_LEN = 256

GROUP_WIDTH = D_MODEL // 4
MIX_WIDTH = 4 * GROUP_WIDTH

HG_HEADS = 4
HG_DK = GROUP_WIDTH // HG_HEADS
HG_DV = GROUP_WIDTH // HG_HEADS
HG_LB_FLOOR = 1e-20
GLA_HEADS = 4
GLA_DK = GROUP_WIDTH // (2 * GLA_HEADS)
GLA_DV = GROUP_WIDTH // GLA_HEADS
GLA_RANK = 16
GLA_TAU = 16.0
DA_HEADS = 4
DA_HD = GROUP_WIDTH // DA_HEADS
DA_ROT = DA_HD // 4
ROPE_THETA = 500000.0
DA_PATTERNS = ((128, 1), (512, 4), (2048, 16))
MASK_VALUE = -1e30
CONV_CH = GROUP_WIDTH
CONV_K = 31
X_HEADS = 4
X_HD = D_MODEL // X_HEADS
D_FF = ((int(8 * D_MODEL / 3) + 255) // 256) * 256

CHUNK = 64
EPS = 1e-6

IN_SPLITS = (
    HG_HEADS * HG_DK,
    HG_HEADS * HG_DK,
    HG_HEADS * HG_DV,
    HG_HEADS * HG_DV,
    GLA_HEADS * GLA_DK,
    GLA_HEADS * GLA_DK,
    GLA_HEADS * GLA_DV,
    GLA_RANK,
    GLA_HEADS * GLA_DV,
    DA_HEADS * DA_HD,
    DA_HEADS * DA_HD,
    DA_HEADS * DA_HD,
    CONV_CH,
    CONV_CH,
)
IN_WIDTH = sum(IN_SPLITS)

kernel_name = "hymba_style_hybrid_hgrn2_gla_dilated_conformer"


def rms_norm(x, g):
    xf = x.astype(jnp.float32)
    y = xf * lax.rsqrt(jnp.mean(xf * xf, axis=-1, keepdims=True) + EPS)
    return (y * g.astype(jnp.float32)).astype(x.dtype)


def swiglu_ffn(x, w_up, w_down):
    gate, up = jnp.split(x @ w_up, 2, axis=-1)
    return (jax.nn.silu(gate) * up) @ w_down


def chunked_gated_linear_attention(q, k, v, log_g):
    B, S, H, DK = q.shape
    DV = v.shape[-1]
    n = S // CHUNK

    def to_chunks(t):
        return t.astype(jnp.float32).reshape(B, n, CHUNK, H, t.shape[-1]).transpose(1, 0, 3, 2, 4)

    causal = jnp.tril(jnp.ones((CHUNK, CHUNK), dtype=bool))[:, :, None]

    def step(state, inp):
        qc, kc, vc, gc = inp
        b = jnp.cumsum(gc, axis=2)
        rel = b[:, :, :, None, :] - b[:, :, None, :, :]
        decay = jnp.where(causal, jnp.exp(jnp.where(causal, rel, 0.0)), 0.0)
        scores = jnp.einsum('bhtk,bhsk,bhtsk->bhts', qc, kc, decay)
        o = (jnp.einsum('bhts,bhsv->bhtv', scores, vc)
             + jnp.einsum('bhtk,bhkv->bhtv', qc * jnp.exp(b), state))
        b_last = b[:, :, -1:, :]
        state = (state * jnp.exp(b_last[:, :, 0, :, None])
                 + jnp.einsum('bhsk,bhsv->bhkv', kc * jnp.exp(b_last - b), vc))
        return state, o

    state0 = jnp.zeros((B, H, DK, DV), jnp.float32)
    _, o = lax.scan(step, state0, (to_chunks(q), to_chunks(k), to_chunks(v), to_chunks(log_g)))
    return o.transpose(1, 0, 3, 2, 4).reshape(B, S, H, DV).astype(v.dtype)


def hgrn2_mixer(q, f_pre, i, g, lb, out_gain):
    B, S, _ = q.shape
    q = q.reshape(B, S, HG_HEADS, HG_DK)
    z = f_pre.reshape(B, S, HG_HEADS, HG_DK).astype(jnp.float32)
    lb = lb.reshape(HG_HEADS, HG_DK)
    log_f = jnp.logaddexp(jnp.log(jnp.maximum(lb, HG_LB_FLOOR)),
                          jnp.log1p(-lb) + jax.nn.log_sigmoid(z))
    k = -jnp.expm1(log_f)
    v = i.reshape(B, S, HG_HEADS, HG_DV)
    o = chunked_gated_linear_attention(q, k, v, log_f)
    o = rms_norm(o, out_gain.reshape(HG_HEADS, HG_DV)) * jax.nn.silu(g.reshape(B, S, HG_HEADS, HG_DV))
    return o.reshape(B, S, HG_HEADS * HG_DV)


def gla_mixer(q, k, v, lr, r, gate_w, gate_b, out_gain):
    B, S, _ = q.shape
    q = q.reshape(B, S, GLA_HEADS, GLA_DK) * (GLA_DK ** -0.5)
    k = k.reshape(B, S, GLA_HEADS, GLA_DK)
    v = v.reshape(B, S, GLA_HEADS, GLA_DV)
    log_a = jax.nn.log_sigmoid((lr @ gate_w + gate_b).astype(jnp.float32)) / GLA_TAU
    log_a = log_a.reshape(B, S, GLA_HEADS, GLA_DK)
    o = chunked_gated_linear_attention(q, k, v, log_a)
    o = rms_norm(o, out_gain.reshape(GLA_HEADS, GLA_DV)) * jax.nn.silu(r.reshape(B, S, GLA_HEADS, GLA_DV))
    return o.reshape(B, S, GLA_HEADS * GLA_DV)


def partial_rope(t, cos, sin):
    half = DA_ROT // 2
    t1 = t[..., :half]
    t2 = t[..., half:DA_ROT]
    return jnp.concatenate([t1 * cos - t2 * sin, t2 * cos + t1 * sin, t[..., DA_ROT:]], axis=-1)


def dilated_branch(q, k, v, dilation, steps):
    B, S, H, hd = q.shape
    L = S // dilation
    n_blk = -(-L // steps)
    Lp = n_blk * steps

    def to_blocks(t):
        t = t.reshape(B, L, dilation, H, hd).transpose(0, 2, 3, 1, 4)
        t = jnp.pad(t, ((0, 0), (0, 0), (0, 0), (0, Lp - L), (0, 0)))
        return t.reshape(B, dilation, H, n_blk, steps, hd)

    def with_prev(t):
        prev = jnp.pad(t, ((0, 0), (0, 0), (0, 0), (1, 0), (0, 0), (0, 0)))[:, :, :, :-1]
        return jnp.concatenate([prev, t], axis=4)

    qb = to_blocks(q)
    kb = with_prev(to_blocks(k))
    vb = with_prev(to_blocks(v))
    s = jnp.einsum('bdhnqe,bdhnke->bdhnqk', qb, kb).astype(jnp.float32) * (hd ** -0.5)
    qi = jnp.arange(steps)[:, None]
    kj = jnp.arange(2 * steps)[None, :]
    steps_back = qi + steps - kj
    key_pos = jnp.arange(n_blk)[:, None, None] * steps - steps + kj
    valid = (steps_back >= 0) & (steps_back <= steps) & (key_pos >= 0)
    s = jnp.where(valid, s, MASK_VALUE)
    lse = jax.nn.logsumexp(s, axis=-1)
    p = jnp.exp(s - lse[..., None])
    o = jnp.einsum('bdhnqk,bdhnke->bdhnqe', p, vb.astype(jnp.float32))
    o = o.reshape(B, dilation, H, Lp, hd)[:, :, :, :L].transpose(0, 3, 1, 2, 4).reshape(B, S, H, hd)
    lse = lse.reshape(B, dilation, H, Lp)[..., :L].transpose(0, 3, 1, 2).reshape(B, S, H)
    return o, lse


def dilated_attention_mixer(q, k, v, cos, sin):
    B, S, _ = q.shape
    q = partial_rope(q.reshape(B, S, DA_HEADS, DA_HD), cos, sin)
    k = partial_rope(k.reshape(B, S, DA_HEADS, DA_HD), cos, sin)
    v = v.reshape(B, S, DA_HEADS, DA_HD)
    outs, lses = [], []
    for window, dilation in DA_PATTERNS:
        o, lse = dilated_branch(q, k, v, dilation, window // dilation)
        outs.append(o)
        lses.append(lse)
    weights = jax.nn.softmax(jnp.stack(lses, axis=0), axis=0)
    o = jnp.sum(weights[..., None] * jnp.stack(outs, axis=0), axis=0)
    return o.reshape(B, S, DA_HEADS * DA_HD).astype(q.dtype)


def conformer_conv_mixer(a, gate, conv_w, conv_b, ln_g, ln_b):
    u = a * jax.nn.sigmoid(gate)
    y = lax.conv_general_dilated(
        u, conv_w[:, None, :].astype(u.dtype), window_strides=(1,),
        padding=[(CONV_K - 1, 0)], dimension_numbers=('NWC', 'WIO', 'NWC'),
        feature_group_count=CONV_CH) + conv_b
    yf = y.astype(jnp.float32)
    mu = jnp.mean(yf, axis=-1, keepdims=True)
    var = jnp.mean(jnp.square(yf - mu), axis=-1, keepdims=True)
    yn = (yf - mu) * lax.rsqrt(var + EPS) * ln_g.astype(jnp.float32) + ln_b.astype(jnp.float32)
    return jax.nn.silu(yn).astype(a.dtype)


def memory_cross_attention(h, m, wq, wkv, wo):
    B, S, _ = h.shape
    M = m.shape[1]
    q = (h @ wq).reshape(B, S, X_HEADS, X_HD)
    k, v = jnp.split(m @ wkv, 2, axis=-1)
    k = k.reshape(B, M, X_HEADS, X_HD)
    v = v.reshape(B, M, X_HEADS, X_HD)
    s = jnp.einsum('bshd,bmhd->bhsm', q, k).astype(jnp.float32) * (X_HD ** -0.5)
    p = jax.nn.softmax(s, axis=-1)
    o = jnp.einsum('bhsm,bmhd->bshd', p.astype(v.dtype), v).reshape(B, S, D_MODEL)
    return o @ wo


def setup_inputs(seed: int = 0) -> dict:
    key = jax.random.key(seed)
    ks = iter(jax.random.split(key, 40))

    def nrm(shape, scale):
        return jax.random.normal(next(ks), shape, jnp.float32) * scale

    def gain(shape):
        return 1.0 + 0.02 * jax.random.normal(next(ks), shape, jnp.float32)

    offsets = jax.random.randint(next(ks), (BATCH, 1), 0, 1024, dtype=jnp.int32)
    positions = offsets + jnp.arange(SEQ, dtype=jnp.int32)[None, :]
    return {
        "x": nrm((BATCH, SEQ, D_MODEL), 1.0),
        "mem": nrm((BATCH, MEM_LEN, D_MODEL), 1.0),
        "positions": positions,
        "hgrn_lb_logits": nrm((DEPTH, HG_HEADS * HG_DK), 0.5),
        "ffn1_norm": gain((DEPTH, D_MODEL)),
        "ffn1_w_up": nrm((DEPTH, D_MODEL, 2 * D_FF), D_MODEL ** -0.5),
        "ffn1_w_down": nrm((DEPTH, D_FF, D_MODEL), D_FF ** -0.5),
        "mix_norm": gain((DEPTH, D_MODEL)),
        "w_in": nrm((DEPTH, D_MODEL, IN_WIDTH), D_MODEL ** -0.5),
        "hgrn_out_norm": gain((DEPTH, HG_HEADS * HG_DV)),
        "gla_gate_w": nrm((DEPTH, GLA_RANK, GLA_HEADS * GLA_DK), GLA_RANK ** -0.5),
        "gla_gate_b": nrm((DEPTH, GLA_HEADS * GLA_DK), 0.1),
        "gla_out_norm": gain((DEPTH, GLA_HEADS * GLA_DV)),
        "conv_w": nrm((DEPTH, CONV_K, CONV_CH), CONV_K ** -0.5),
        "conv_b": nrm((DEPTH, CONV_CH), 0.02),
        "conv_ln_g": gain((DEPTH, CONV_CH)),
        "conv_ln_b": nrm((DEPTH, CONV_CH), 0.02),
        "w_out": nrm((DEPTH, MIX_WIDTH, D_MODEL), MIX_WIDTH ** -0.5),
        "cross_norm": gain((DEPTH, D_MODEL)),
        "mem_norm": gain((DEPTH, D_MODEL)),
        "cross_wq": nrm((DEPTH, D_MODEL, D_MODEL), D_MODEL ** -0.5),
        "cross_wkv": nrm((DEPTH, D_MODEL, 2 * D_MODEL), D_MODEL ** -0.5),
        "cross_wo": nrm((DEPTH, D_MODEL, D_MODEL), D_MODEL ** -0.5),
        "ffn2_norm": gain((DEPTH, D_MODEL)),
        "ffn2_w_up": nrm((DEPTH, D_MODEL, 2 * D_FF), D_MODEL ** -0.5),
        "ffn2_w_down": nrm((DEPTH, D_FF, D_MODEL), D_FF ** -0.5),
        "final_norm": gain((D_MODEL,)),
    }


def reference(x, mem, positions, hgrn_lb_logits, ffn1_norm, ffn1_w_up, ffn1_w_down,
              mix_norm, w_in, hgrn_out_norm, gla_gate_w, gla_gate_b, gla_out_norm,
              conv_w, conv_b, conv_ln_g, conv_ln_b, w_out, cross_norm, mem_norm,
              cross_wq, cross_wkv, cross_wo, ffn2_norm, ffn2_w_up, ffn2_w_down, final_norm):
    inv_freq = jnp.power(jnp.float32(ROPE_THETA),
                         -jnp.arange(0, DA_ROT, 2, dtype=jnp.float32) / DA_ROT)
    ang = positions.astype(jnp.float32)[..., None] * inv_freq
    cos = jnp.cos(ang)[:, :, None, :].astype(x.dtype)
    sin = jnp.sin(ang)[:, :, None, :].astype(x.dtype)

    p_lb = jax.nn.softmax(hgrn_lb_logits.astype(jnp.float32), axis=0)
    lower_bounds = jnp.cumsum(p_lb, axis=0) - p_lb[0:1]

    split_points = [int(i) for i in np.cumsum(IN_SPLITS)[:-1]]

    for l in range(DEPTH):
        x = x + 0.5 * swiglu_ffn(rms_norm(x, ffn1_norm[l]), ffn1_w_up[l], ffn1_w_down[l])

        h = rms_norm(x, mix_norm[l])
        (hg_q, hg_f, hg_i, hg_g, gl_q, gl_k, gl_v, gl_lr, gl_r,
         da_q, da_k, da_v, cv_a, cv_g) = jnp.split(h @ w_in[l], split_points, axis=-1)

        o_a = hgrn2_mixer(hg_q, hg_f, hg_i, hg_g, lower_bounds[l], hgrn_out_norm[l])
        o_b = gla_mixer(gl_q, gl_k, gl_v, gl_lr, gl_r, gla_gate_w[l], gla_gate_b[l], gla_out_norm[l])
        o_c = dilated_attention_mixer(da_q, da_k, da_v, cos, sin)
        o_d = conformer_conv_mixer(cv_a, cv_g, conv_w[l], conv_b[l], conv_ln_g[l], conv_ln_b[l])
        x = x + jnp.concatenate([o_a, o_b, o_c, o_d], axis=-1) @ w_out[l]

        x = x + memory_cross_attention(rms_norm(x, cross_norm[l]), rms_norm(mem, mem_norm[l]),
                                       cross_wq[l], cross_wkv[l], cross_wo[l])

        x = x + 0.5 * swiglu_ffn(rms_norm(x, ffn2_norm[l]), ffn2_w_up[l], ffn2_w_down[l])

    return rms_norm(x, final_norm)
```

```python
import functools
import math

import numpy as np
import jax
import jax.numpy as jnp
from jax import lax
from jax.experimental import pallas as pl
from jax.experimental.pallas import tpu as pltpu

F32 = jnp.float32
BF16 = jnp.bfloat16

D_MODEL = 1024
GROUP_WIDTH = D_MODEL // 4
HEADS = 4
HG_DK = GROUP_WIDTH // HEADS
GLA_DK = GROUP_WIDTH // (2 * HEADS)
HEAD_DV = GROUP_WIDTH // HEADS
GLA_RANK = 16
GLA_TAU = 16.0
HG_LB_FLOOR = 1e-20
DA_HD = GROUP_WIDTH // HEADS
DA_ROT = DA_HD // 4
ROPE_THETA = 500000.0
DA_PATTERNS = ((128, 1), (512, 4), (2048, 16))
MASK_VALUE = -1e30
CONV_K = 31
X_HD = D_MODEL // HEADS
D_FF = ((int(8 * D_MODEL / 3) + 255) // 256) * 256
EPS = 1e-6

LANES = 128
ROW_TILE = 512
FF_CHUNK = D_FF // 4
GLA_CHUNK = 64
DA_STEPS = 128
CONV_TILE = 256
CONV_PAD = 32
VMEM_LIMIT = 56 * 1024 * 1024

PA_W = 4 * GROUP_WIDTH
PB_W = 2 * HEADS * GLA_DK + 2 * GROUP_WIDTH + LANES
PC_W = 3 * GROUP_WIDTH
PD_W = 2 * GROUP_WIDTH


def _dot(a, b):
    return jnp.dot(a, b, preferred_element_type=F32)


def _dot_t(a, b):
    return lax.dot_general(a, b, (((1,), (1,)), ((), ())), preferred_element_type=F32)


def _tdot(a, b):
    return lax.dot_general(a, b, (((0,), (0,)), ((), ())), preferred_element_type=F32)


def _rms(x, g):
    ms = jnp.mean(x * x, axis=-1, keepdims=True)
    return x * lax.rsqrt(ms + EPS) * g


def _split3(x):
    hi = x.astype(BF16)
    r1 = x - hi.astype(F32)
    mid = r1.astype(BF16)
    lo = (r1 - mid.astype(F32)).astype(BF16)
    return hi, mid, lo


def _const_spec(shape):
    nd = len(shape)
    return pl.BlockSpec(shape, lambda *_: (0,) * nd, pipeline_mode=pl.Buffered(1))


def _params(sem):
    return pltpu.CompilerParams(dimension_semantics=sem, vmem_limit_bytes=VMEM_LIMIT)


def _ffn_body(x_ref, g_ref, wup_ref, wd_ref):
    x = x_ref[...]
    h = _rms(x, g_ref[...]).astype(BF16)
    acc = jnp.zeros_like(x)
    for c in range(D_FF // FF_CHUNK):
        lo = c * FF_CHUNK
        gate = _dot(h, wup_ref[:, lo:lo + FF_CHUNK])
        up = _dot(h, wup_ref[:, D_FF + lo:D_FF + lo + FF_CHUNK])
        a = (gate * jax.nn.sigmoid(gate) * up).astype(BF16)
        acc = acc + _dot(a, wd_ref[lo:lo + FF_CHUNK, :])
    return x + 0.5 * acc


def _ffn_kernel(x_ref, g_ref, wup_ref, wd_ref, o_ref):
    o_ref[...] = _ffn_body(x_ref, g_ref, wup_ref, wd_ref)


def _ffn_final_kernel(x_ref, g_ref, wup_ref, wd_ref, fg_ref, o_ref):
    o_ref[...] = _rms(_ffn_body(x_ref, g_ref, wup_ref, wd_ref), fg_ref[...])


def _ffn(x2, g, wup, wd, final_g=None):
    n = x2.shape[0]
    row = pl.BlockSpec((ROW_TILE, D_MODEL), lambda i: (i, 0))
    in_specs = [row, _const_spec((1, D_MODEL)), _const_spec(wup.shape), _const_spec(wd.shape)]
    args = [x2, g, wup, wd]
    body = _ffn_kernel
    if final_g is not None:
        in_specs.append(_const_spec((1, D_MODEL)))
        args.append(final_g)
        body = _ffn_final_kernel
    return pl.pallas_call(
        body, grid=(n // ROW_TILE,), in_specs=in_specs, out_specs=row,
        out_shape=jax.ShapeDtypeStruct(x2.shape, F32),
        compiler_params=_params(("parallel",)), name="ffn")(*args)


def _inproj_kernel(x_ref, g_ref, w_ref, pa_ref, pb_ref, pc_ref, pd_ref):
    h = _rms(x_ref[...], g_ref[...]).astype(BF16)
    off = 0
    for ref, width in ((pa_ref, PA_W), (pb_ref, PB_W), (pc_ref, PC_W), (pd_ref, PD_W)):
        ref[...] = _dot(h, w_ref[:, off:off + width])
        off += width


def _inproj(x2, g, w):
    n = x2.shape[0]
    widths = (PA_W, PB_W, PC_W, PD_W)
    return pl.pallas_call(
        _inproj_kernel, grid=(n // ROW_TILE,),
        in_specs=[pl.BlockSpec((ROW_TILE, D_MODEL), lambda i: (i, 0)),
                  _const_spec((1, D_MODEL)), _const_spec(w.shape)],
        out_specs=[pl.BlockSpec((ROW_TILE, wd), lambda i: (i, 0)) for wd in widths],
        out_shape=[jax.ShapeDtypeStruct((n, wd), F32) for wd in widths],
        compiler_params=_params(("parallel",)), name="inproj")(x2, g, w)


def _outproj_kernel(x_ref, oa_ref, ob_ref, oc_ref, od_ref, w_ref, o_ref):
    acc = x_ref[...]
    for j, ref in enumerate((oa_ref, ob_ref, oc_ref, od_ref)):
        acc = acc + _dot(ref[...], w_ref[j * GROUP_WIDTH:(j + 1) * GROUP_WIDTH, :])
    o_ref[...] = acc


def _outproj(x2, outs, w):
    n = x2.shape[0]
    row = pl.BlockSpec((ROW_TILE, D_MODEL), lambda i: (i, 0))
    grp = pl.BlockSpec((ROW_TILE, GROUP_WIDTH), lambda i: (i, 0))
    return pl.pallas_call(
        _outproj_kernel, grid=(n // ROW_TILE,),
        in_specs=[row, grp, grp, grp, grp, _const_spec(w.shape)], out_specs=row,
        out_shape=jax.ShapeDtypeStruct(x2.shape, F32),
        compiler_params=_params(("parallel",)), name="outproj")(x2, *outs, w)


def _memkv_kernel(m_ref, g_ref, w_ref, kv_ref):
    h = _rms(m_ref[...], g_ref[...]).astype(BF16)
    kv_ref[...] = _dot(h, w_ref[...]).astype(BF16)


def _memkv(mem2, g, w):
    n = mem2.shape[0]
    return pl.pallas_call(
        _memkv_kernel, grid=(n // ROW_TILE,),
        in_specs=[pl.BlockSpec((ROW_TILE, D_MODEL), lambda i: (i, 0)),
                  _const_spec((1, D_MODEL)), _const_spec(w.shape)],
        out_specs=pl.BlockSpec((ROW_TILE, 2 * D_MODEL), lambda i: (i, 0)),
        out_shape=jax.ShapeDtypeStruct((n, 2 * D_MODEL), BF16),
        compiler_params=_params(("parallel",)), name="memkv")(mem2, g, w)


def _cross_kernel(x_ref, g_ref, wq_ref, kv_ref, wo_ref, o_ref):
    x = x_ref[...]
    h = _rms(x, g_ref[...]).astype(BF16)
    q = _dot(h, wq_ref[...]).astype(BF16)
    heads = []
    for hd in range(HEADS):
        lo = hd * X_HD
        s = _dot_t(q[:, lo:lo + X_HD], kv_ref[:, lo:lo + X_HD]) * (X_HD ** -0.5)
        s = s - jnp.max(s, axis=-1, keepdims=True)
        p = jnp.exp(s)
        p = p / jnp.sum(p, axis=-1, keepdims=True)
        heads.append(_dot(p.astype(BF16), kv_ref[:, D_MODEL + lo:D_MODEL + lo + X_HD]).astype(BF16))
    o = jnp.concatenate(heads, axis=-1)
    o_ref[...] = x + _dot(o, wo_ref[...])


def _cross(x2, g, wq, kv, wo, seq, mem_len):
    n = x2.shape[0]
    tiles_per_seq = seq // ROW_TILE
    row = pl.BlockSpec((ROW_TILE, D_MODEL), lambda i: (i, 0))
    return pl.pallas_call(
        _cross_kernel, grid=(n // ROW_TILE,),
        in_specs=[row, _const_spec((1, D_MODEL)), _const_spec(wq.shape),
                  pl.BlockSpec((mem_len, 2 * D_MODEL), lambda i: (i // tiles_per_seq, 0)),
                  _const_spec(wo.shape)],
        out_specs=row, out_shape=jax.ShapeDtypeStruct(x2.shape, F32),
        compiler_params=_params(("parallel",)), name="cross")(x2, g, wq, kv, wo)


def _gla_constants(dk):
    c, h = GLA_CHUNK, HEADS
    w, wv = h * dk, h * HEAD_DV
    r = np.arange(c)
    blocks = [(r[:, None] >= r[None, :])]
    masks = [(r[:, None] == r[None, :])]
    m = 1
    while m < c:
        ref = (r // (2 * m)) * (2 * m) + m - 1
        upper = (r % (2 * m)) >= m
        rp = r[None, :]
        rng_up = (rp > ref[:, None]) & (rp <= r[:, None])
        rng_lo = (rp > r[:, None]) & (rp <= ref[:, None])
        blocks.append(np.where(upper[:, None], rng_up, rng_lo))
        same = (r[:, None] // (2 * m)) == (r[None, :] // (2 * m))
        masks.append(same & upper[:, None] & (~upper)[None, :])
        m *= 2
    nmat = np.concatenate(blocks, axis=0).astype(np.float32)
    lvl = np.stack([np.tile(mk, (1, h)) for mk in masks]).astype(np.float32)
    rows = np.arange(h * c)[:, None] // c
    hm_k = (rows == (np.arange(w)[None, :] // dk)).astype(np.float32)
    hm_v = (rows == (np.arange(wv)[None, :] // HEAD_DV)).astype(np.float32)
    bm_t = ((np.arange(wv)[:, None] // HEAD_DV) == (np.arange(w)[None, :] // dk)).astype(np.float32)
    ones_blk = ((np.arange(wv)[:, None] // HEAD_DV) == (np.arange(wv)[None, :] // HEAD_DV))
    return (jnp.asarray(nmat, BF16), jnp.asarray(lvl, F32), jnp.asarray(hm_k, BF16),
            jnp.asarray(hm_v, BF16), jnp.asarray(bm_t, F32), jnp.asarray(ones_blk.astype(np.float32), BF16))


def _gla_recurrence(seq, q_ref, k_ref, v_ref, g_ref, o_ref, st_ref,
                    nmat_ref, lvl_ref, hmk_ref, hmv_ref, bmt_ref):
    c, h = GLA_CHUNK, HEADS
    n_lvl = lvl_ref.shape[0]
    st_ref[...] = jnp.zeros_like(st_ref)

    def step(ci, carry):
        r0 = pl.multiple_of(ci * c, c)
        q = q_ref[pl.ds(r0, c), :]
        k = k_ref[pl.ds(r0, c), :]
        v = v_ref[pl.ds(r0, c), :]
        g = g_ref[pl.ds(r0, c), :]
        nmat = nmat_ref[...]
        g_hi, g_mid, g_lo = _split3(g)
        sums = _dot(nmat, g_hi) + _dot(nmat, g_mid) + _dot(nmat, g_lo)
        b = sums[0:c]
        hmk = hmk_ref[...]
        acc = jnp.zeros((c, h * c), F32)
        for li in range(n_lvl):
            if li == 0:
                qt, kt = q, k
            else:
                e = jnp.exp(sums[li * c:(li + 1) * c])
                qt, kt = q * e, k * e
            kst = jnp.concatenate([kt.astype(BF16)] * h, axis=0) * hmk
            acc = acc + _dot_t(qt.astype(BF16), kst) * lvl_ref[li]
        vb = v.astype(BF16)
        vst = jnp.concatenate([vb] * h, axis=0) * hmv_ref[...]
        o = _dot(acc.astype(BF16), vst)
        st = st_ref[...]
        o = o + _dot_t((q * jnp.exp(b)).astype(BF16), st.astype(BF16))
        b_last = b[c - 1:c, :]
        kh = (k * jnp.exp(b_last - b)).astype(BF16)
        st_ref[...] = st * jnp.exp(b_last) + _tdot(vb, kh) * bmt_ref[...]
        o_ref[pl.ds(r0, c), :] = o
        return carry

    lax.fori_loop(0, seq // c, step, 0)


def _head_rms_gate(o, gain, gate, ones_blk):
    sq = o * o
    hi = sq.astype(BF16)
    lo = (sq - hi.astype(F32)).astype(BF16)
    ms = (_dot(hi, ones_blk) + _dot(lo, ones_blk)) * (1.0 / HEAD_DV)
    return o * lax.rsqrt(ms + EPS) * gain * (gate * jax.nn.sigmoid(gate))


def _hgrn_kernel(p_ref, lb_ref, gain_ref, nmat_ref, lvl_ref, hmk_ref, hmv_ref, bmt_ref, ones_ref,
                 o_ref, q_s, k_s, g_s, o_s, st_s):
    seq, gw = o_ref.shape
    z = p_ref[:, gw:2 * gw]
    lb = lb_ref[...]
    lbf = jnp.maximum(lb, HG_LB_FLOOR)
    e = jnp.exp(-jnp.abs(z))
    inv = 1.0 / (1.0 + e)
    sig = jnp.where(z >= 0, inv, e * inv)
    nsig = jnp.where(z >= 0, e * inv, inv)
    f = lbf + (1.0 - lb) * sig
    g_s[...] = jnp.log(f)
    k_s[...] = (1.0 - lb) * nsig - (lbf - lb)
    q_s[...] = p_ref[:, 0:gw]
    _gla_recurrence(seq, q_s, k_s, p_ref.at[:, 2 * gw:3 * gw], g_s, o_s, st_s,
                    nmat_ref, lvl_ref, hmk_ref, hmv_ref, bmt_ref)
    o_ref[...] = _head_rms_gate(o_s[...], gain_ref[...], p_ref[:, 3 * gw:4 * gw],
                                ones_ref[...]).astype(o_ref.dtype)


def _gla_kernel(p_ref, gw_ref, gb_ref, gain_ref, nmat_ref, lvl_ref, hmk_ref, hmv_ref, bmt_ref,
                ones_ref, o_ref, q_s, g_s, o_s, st_s):
    seq, gw = o_ref.shape
    wk = HEADS * GLA_DK
    lr = p_ref[:, 2 * wk + 2 * gw:2 * wk + 2 * gw + LANES]
    lr_hi = lr.astype(BF16)
    lr_lo = (lr - lr_hi.astype(F32)).astype(BF16)
    w = gw_ref[...]
    w_hi = w.astype(BF16)
    w_lo = (w - w_hi.astype(F32)).astype(BF16)
    y = _dot(lr_hi, w_hi) + _dot(lr_hi, w_lo) + _dot(lr_lo, w_hi) + gb_ref[...]
    g_s[...] = (jnp.minimum(y, 0.0) - jnp.log(1.0 + jnp.exp(-jnp.abs(y)))) * (1.0 / GLA_TAU)
    q_s[...] = p_ref[:, 0:wk] * (GLA_DK ** -0.5)
    _gla_recurrence(seq, q_s, p_ref.at[:, wk:2 * wk], p_ref.at[:, 2 * wk:2 * wk + gw], g_s, o_s, st_s,
                    nmat_ref, lvl_ref, hmk_ref, hmv_ref, bmt_ref)
    o_ref[...] = _head_rms_gate(o_s[...], gain_ref[...], p_ref[:, 2 * wk + gw:2 * wk + 2 * gw],
                                ones_ref[...]).astype(o_ref.dtype)


def _seq_spec(seq, width):
    return pl.BlockSpec((seq, width), lambda b: (b, 0))


def _hgrn(pa, lb, gain, batch, seq):
    consts = _gla_constants(HG_DK)
    w = HEADS * HG_DK
    return pl.pallas_call(
        _hgrn_kernel, grid=(batch,),
        in_specs=[_seq_spec(seq, PA_W), _const_spec((1, w)), _const_spec((1, GROUP_WIDTH))]
                 + [_const_spec(cst.shape) for cst in consts],
        out_specs=_seq_spec(seq, GROUP_WIDTH),
        out_shape=jax.ShapeDtypeStruct((batch * seq, GROUP_WIDTH), BF16),
        scratch_shapes=[pltpu.VMEM((seq, w), F32), pltpu.VMEM((seq, w), F32), pltpu.VMEM((seq, w), F32),
                        pltpu.VMEM((seq, GROUP_WIDTH), F32), pltpu.VMEM((GROUP_WIDTH, w), F32)],
        compiler_params=_params(("parallel",)), name="hgrn")(pa, lb, gain, *consts)


def _gla(pb, gate_w, gate_b, gain, batch, seq):
    consts = _gla_constants(GLA_DK)
    w = HEADS * GLA_DK
    return pl.pallas_call(
        _gla_kernel, grid=(batch,),
        in_specs=[_seq_spec(seq, PB_W), _const_spec((LANES, w)), _const_spec((1, w)),
                  _const_spec((1, GROUP_WIDTH))] + [_const_spec(cst.shape) for cst in consts],
        out_specs=_seq_spec(seq, GROUP_WIDTH),
        out_shape=jax.ShapeDtypeStruct((batch * seq, GROUP_WIDTH), BF16),
        scratch_shapes=[pltpu.VMEM((seq, w), F32), pltpu.VMEM((seq, w), F32),
                        pltpu.VMEM((seq, GROUP_WIDTH), F32), pltpu.VMEM((GROUP_WIDTH, w), F32)],
        compiler_params=_params(("parallel",)), name="gla")(pb, gate_w, gate_b, gain, *consts)


def _dilated_kernel(p_ref, cos_ref, sin_ref, o_ref, q_s, k_s, v_s, op_s, lse_s):
    seq, gw = o_ref.shape
    n_half = gw // LANES
    heads_per_half = LANES // DA_HD
    lane = lax.broadcasted_iota(jnp.int32, (1, LANES), 1)
    in_head = lane & (DA_HD - 1)
    first_half = in_head < (DA_ROT // 2)
    cos, sin = cos_ref[...], sin_ref[...]

    def rope(t):
        partner = jnp.where(first_half, pltpu.roll(t, LANES - DA_ROT // 2, 1),
                            pltpu.roll(t, DA_ROT // 2, 1))
        return t * cos + partner * sin

    for hf in range(n_half):
        lo = hf * LANES
        q_s[hf] = rope(p_ref[:, lo:lo + LANES]) * (DA_HD ** -0.5)
        k_s[hf] = rope(p_ref[:, gw + lo:gw + lo + LANES])
        v_s[hf] = p_ref[:, 2 * gw + lo:2 * gw + lo + LANES]

    nq = DA_STEPS
    qi = lax.broadcasted_iota(jnp.int32, (nq, 2 * nq), 0)
    kj = lax.broadcasted_iota(jnp.int32, (nq, 2 * nq), 1)
    cur_ok = (kj >= nq) & (kj - nq <= qi)
    head_of_lane = lane >> int(math.log2(DA_HD))

    for pi, (window, dil) in enumerate(DA_PATTERNS):
        n_blk = seq // (dil * nq)

        def block(i, carry, pi=pi, dil=dil, n_blk=n_blk):
            r = i // n_blk
            n = i % n_blk
            cur0 = r + n * (nq * dil)
            prev0 = r + jnp.maximum(n - 1, 0) * (nq * dil)

            def rows(start):
                if dil == 1:
                    return pl.ds(pl.multiple_of(start, nq), nq)
                return pl.ds(start, nq, stride=dil)

            first_row = qi + jnp.where(n > 0, 0, nq)
            valid = cur_ok | ((kj < nq) & (kj >= first_row))
            for hf in range(n_half):
                q = q_s[hf, rows(cur0), :]
                kwin = jnp.concatenate([k_s[hf, rows(prev0), :], k_s[hf, rows(cur0), :]], axis=0).astype(BF16)
                vwin = jnp.concatenate([v_s[hf, rows(prev0), :], v_s[hf, rows(cur0), :]], axis=0).astype(BF16)
                o_acc = jnp.zeros((nq, LANES), F32)
                lse_acc = jnp.zeros((nq, LANES), F32)
                for hd in range(heads_per_half):
                    hm = head_of_lane == hd
                    s = _dot_t(jnp.where(hm, q, 0.0).astype(BF16), kwin)
                    s = jnp.where(valid, s, MASK_VALUE)
                    m = jnp.max(s, axis=-1, keepdims=True)
                    p = jnp.exp(s - m)
                    l = jnp.sum(p, axis=-1, keepdims=True)
                    oh = _dot(p.astype(BF16), vwin) / l
                    o_acc = jnp.where(hm, oh, o_acc)
                    lse_acc = jnp.where(hm, m + jnp.log(l), lse_acc)
                op_s[pi, hf, rows(cur0), :] = o_acc
                lse_s[pi, hf, rows(cur0), :] = lse_acc
            return carry

        lax.fori_loop(0, dil * n_blk, block, 0)

    for hf in range(n_half):
        lses = [lse_s[pi, hf] for pi in range(len(DA_PATTERNS))]
        mx = functools.reduce(jnp.maximum, lses)
        ws = [jnp.exp(l - mx) for l in lses]
        num = sum(wt * op_s[pi, hf] for pi, wt in enumerate(ws))
        o_ref[:, hf * LANES:(hf + 1) * LANES] = (num / sum(ws)).astype(o_ref.dtype)


def _dilated(pc, cos_t, sin_t, batch, seq):
    gw = GROUP_WIDTH
    n_pat = len(DA_PATTERNS)
    n_half = gw // LANES
    return pl.pallas_call(
        _dilated_kernel, grid=(batch,),
        in_specs=[_seq_spec(seq, PC_W), _seq_spec(seq, LANES), _seq_spec(seq, LANES)],
        out_specs=_seq_spec(seq, gw),
        out_shape=jax.ShapeDtypeStruct((batch * seq, gw), BF16),
        scratch_shapes=[pltpu.VMEM((n_half, seq, LANES), F32)] * 3
                       + [pltpu.VMEM((n_pat, n_half, seq, LANES), F32)] * 2,
        compiler_params=_params(("parallel",)), name="dilated")(pc, cos_t, sin_t)


def _conv_kernel(p_ref, w_ref, b_ref, lg_ref, lb_ref, o_ref, u_s):
    seq, gw = o_ref.shape
    a = p_ref[:, 0:gw]
    gate = p_ref[:, gw:2 * gw]
    u_s[0:CONV_PAD, :] = jnp.zeros((CONV_PAD, gw), F32)
    u_s[CONV_PAD:CONV_PAD + seq, :] = a * jax.nn.sigmoid(gate)
    w = w_ref[...]
    first = CONV_PAD - (CONV_K - 1)
    for t in range(seq // CONV_TILE):
        t0 = t * CONV_TILE
        y = jnp.zeros((CONV_TILE, gw), F32) + b_ref[...]
        for j in range(CONV_K):
            y = y + u_s[t0 + first + j:t0 + first + j + CONV_TILE, :] * w[j:j + 1, :]
        mu = jnp.mean(y, axis=-1, keepdims=True)
        d = y - mu
        var = jnp.mean(d * d, axis=-1, keepdims=True)
        yn = d * lax.rsqrt(var + EPS) * lg_ref[...] + lb_ref[...]
        o_ref[t0:t0 + CONV_TILE, :] = (yn * jax.nn.sigmoid(yn)).astype(o_ref.dtype)


def _conv(pd, w, b, ln_g, ln_b, batch, seq):
    gw = GROUP_WIDTH
    vec = _const_spec((1, gw))
    return pl.pallas_call(
        _conv_kernel, grid=(batch,),
        in_specs=[_seq_spec(seq, PD_W), _const_spec(w.shape), vec, vec, vec],
        out_specs=_seq_spec(seq, gw),
        out_shape=jax.ShapeDtypeStruct((batch * seq, gw), BF16),
        scratch_shapes=[pltpu.VMEM((CONV_PAD + seq, gw), F32)],
        compiler_params=_params(("parallel",)), name="conv")(pd, w, b, ln_g, ln_b)


def _rope_tables(positions):
    half = DA_ROT // 2
    inv_freq = jnp.power(jnp.float32(ROPE_THETA), -jnp.arange(0, DA_ROT, 2, dtype=F32) / DA_ROT)
    ang = positions.astype(F32)[..., None] * inv_freq
    cos, sin = jnp.cos(ang), jnp.sin(ang)
    rest = DA_HD - DA_ROT
    ones = jnp.ones(cos.shape[:-1] + (rest,), F32)
    cos_h = jnp.concatenate([cos, cos, ones], axis=-1)
    sin_h = jnp.concatenate([-sin, sin, 0.0 * ones], axis=-1)
    b, s = positions.shape
    tile = lambda t: jnp.tile(t, (1, 1, LANES // DA_HD)).reshape(b * s, LANES)
    return tile(cos_h), tile(sin_h)


def _relayout_w_in(w_in):
    a_end = PA_W
    wk = HEADS * GLA_DK
    b_qkv = w_in[:, a_end:a_end + 2 * wk + GROUP_WIDTH]
    lr0 = a_end + 2 * wk + GROUP_WIDTH
    b_lr = w_in[:, lr0:lr0 + GLA_RANK]
    b_r = w_in[:, lr0 + GLA_RANK:lr0 + GLA_RANK + GROUP_WIDTH]
    c0 = lr0 + GLA_RANK + GROUP_WIDTH
    rest = w_in[:, c0:]
    pad = jnp.zeros((w_in.shape[0], LANES - GLA_RANK), w_in.dtype)
    return jnp.concatenate([w_in[:, :a_end], b_qkv, b_r, b_lr, pad, rest], axis=1).astype(BF16)


def kernel(x, mem, positions, hgrn_lb_logits, ffn1_norm, ffn1_w_up, ffn1_w_down, mix_norm, w_in, hgrn_out_norm, gla_gate_w, gla_gate_b, gla_out_norm, conv_w, conv_b, conv_ln_g, conv_ln_b, w_out, cross_norm, mem_norm, cross_wq, cross_wkv, cross_wo, ffn2_norm, ffn2_w_up, ffn2_w_down, final_norm):
    batch, seq, d = x.shape
    mem_len = mem.shape[1]
    depth = w_in.shape[0]
    assert d == D_MODEL and seq % (DA_PATTERNS[-1][1] * DA_STEPS) == 0
    assert (batch * seq) % ROW_TILE == 0 and seq % ROW_TILE == 0 and (batch * mem_len) % ROW_TILE == 0

    cos_t, sin_t = _rope_tables(positions)
    p_lb = jax.nn.softmax(hgrn_lb_logits.astype(F32), axis=0)
    lower_bounds = jnp.cumsum(p_lb, axis=0) - p_lb[0:1]

    row = lambda v: v.reshape(1, -1).astype(F32)
    x2 = x.reshape(batch * seq, d)
    mem2 = mem.reshape(batch * mem_len, d)
    for l in range(depth):
        x2 = _ffn(x2, row(ffn1_norm[l]), ffn1_w_up[l].astype(BF16), ffn1_w_down[l].astype(BF16))

        pa, pb, pc, pd = _inproj(x2, row(mix_norm[l]), _relayout_w_in(w_in[l]))
        gate_w = jnp.zeros((LANES, HEADS * GLA_DK), F32).at[:GLA_RANK].set(gla_gate_w[l])
        o_a = _hgrn(pa, row(lower_bounds[l]), row(hgrn_out_norm[l]), batch, seq)
        o_b = _gla(pb, gate_w, row(gla_gate_b[l]), row(gla_out_norm[l]), batch, seq)
        o_c = _dilated(pc, cos_t, sin_t, batch, seq)
        o_d = _conv(pd, conv_w[l], row(conv_b[l]), row(conv_ln_g[l]), row(conv_ln_b[l]), batch, seq)
        x2 = _outproj(x2, (o_a, o_b, o_c, o_d), w_out[l].astype(BF16))

        kv = _memkv(mem2, row(mem_norm[l]), cross_wkv[l].astype(BF16))
        x2 = _cross(x2, row(cross_norm[l]), cross_wq[l].astype(BF16), kv, cross_wo[l].astype(BF16),
                    seq, mem_len)

        last = l == depth - 1
        x2 = _ffn(x2, row(ffn2_norm[l]), ffn2_w_up[l].astype(BF16), ffn2_w_down[l].astype(BF16),
                  final_g=row(final_norm) if last else None)
    return x2.reshape(batch, seq, d)
```

```python
import functools
import math

import numpy as np
import jax
import jax.numpy as jnp
from jax import lax
from jax.experimental import pallas as pl
from jax.experimental.pallas import tpu as pltpu

F32 = jnp.float32
BF16 = jnp.bfloat16

D_MODEL = 1024
GROUP_WIDTH = D_MODEL // 4
HEADS = 4
HG_DK = GROUP_WIDTH // HEADS
GLA_DK = GROUP_WIDTH // (2 * HEADS)
HEAD_DV = GROUP_WIDTH // HEADS
GLA_RANK = 16
GLA_TAU = 16.0
HG_LB_FLOOR = 1e-20
DA_HD = GROUP_WIDTH // HEADS
DA_ROT = DA_HD // 4
ROPE_THETA = 500000.0
DA_PATTERNS = ((128, 1), (512, 4), (2048, 16))
MASK_VALUE = -1e30
CONV_K = 31
X_HD = D_MODEL // HEADS
D_FF = ((int(8 * D_MODEL / 3) + 255) // 256) * 256
EPS = 1e-6

LANES = 128
ROW_TILE = 512
FF_CHUNK = D_FF // 4
GLA_CHUNK = 64
GLA_UNROLL = 4
DA_STEPS = 128
DA_UNROLL = 4
CONV_TILE = 256
CONV_PAD = 32
VMEM_LIMIT = 56 * 1024 * 1024

PA_W = 4 * GROUP_WIDTH
PB_W = 2 * HEADS * GLA_DK + 2 * GROUP_WIDTH + LANES
PC_W = 3 * GROUP_WIDTH
PD_W = 2 * GROUP_WIDTH


def _dot(a, b):
    return jnp.dot(a, b, preferred_element_type=F32)


def _dot_t(a, b):
    return lax.dot_general(a, b, (((1,), (1,)), ((), ())), preferred_element_type=F32)


def _tdot(a, b):
    return lax.dot_general(a, b, (((0,), (0,)), ((), ())), preferred_element_type=F32)


def _rms(x, g):
    ms = jnp.mean(x * x, axis=-1, keepdims=True)
    return x * lax.rsqrt(ms + EPS) * g


def _split3(x):
    hi = x.astype(BF16)
    r1 = x - hi.astype(F32)
    mid = r1.astype(BF16)
    lo = (r1 - mid.astype(F32)).astype(BF16)
    return hi, mid, lo


def _const_spec(shape):
    nd = len(shape)
    return pl.BlockSpec(shape, lambda *_: (0,) * nd, pipeline_mode=pl.Buffered(1))


def _params(sem):
    return pltpu.CompilerParams(dimension_semantics=sem, vmem_limit_bytes=VMEM_LIMIT)


def _ffn_body(x_ref, g_ref, wup_ref, wd_ref):
    x = x_ref[...]
    h = _rms(x, g_ref[...]).astype(BF16)
    acc = jnp.zeros_like(x)
    for c in range(D_FF // FF_CHUNK):
        lo = c * FF_CHUNK
        gate = _dot(h, wup_ref[:, lo:lo + FF_CHUNK])
        up = _dot(h, wup_ref[:, D_FF + lo:D_FF + lo + FF_CHUNK])
        a = (gate * jax.nn.sigmoid(gate) * up).astype(BF16)
        acc = acc + _dot(a, wd_ref[lo:lo + FF_CHUNK, :])
    return x + 0.5 * acc


def _ffn_kernel(x_ref, g_ref, wup_ref, wd_ref, o_ref):
    o_ref[...] = _ffn_body(x_ref, g_ref, wup_ref, wd_ref)


def _ffn_final_kernel(x_ref, g_ref, wup_ref, wd_ref, fg_ref, o_ref):
    o_ref[...] = _rms(_ffn_body(x_ref, g_ref, wup_ref, wd_ref), fg_ref[...])


def _ffn(x2, g, wup, wd, final_g=None):
    n = x2.shape[0]
    row = pl.BlockSpec((ROW_TILE, D_MODEL), lambda i: (i, 0))
    in_specs = [row, _const_spec((1, D_MODEL)), _const_spec(wup.shape), _const_spec(wd.shape)]
    args = [x2, g, wup, wd]
    body = _ffn_kernel
    if final_g is not None:
        in_specs.append(_const_spec((1, D_MODEL)))
        args.append(final_g)
        body = _ffn_final_kernel
    return pl.pallas_call(
        body, grid=(n // ROW_TILE,), in_specs=in_specs, out_specs=row,
        out_shape=jax.ShapeDtypeStruct(x2.shape, F32),
        compiler_params=_params(("parallel",)), name="ffn")(*args)


def _inproj_kernel(x_ref, g_ref, w_ref, pa_ref, pb_ref, pc_ref, pd_ref):
    h = _rms(x_ref[...], g_ref[...]).astype(BF16)
    off = 0
    for ref, width in ((pa_ref, PA_W), (pb_ref, PB_W), (pc_ref, PC_W), (pd_ref, PD_W)):
        ref[...] = _dot(h, w_ref[:, off:off + width])
        off += width


def _inproj(x2, g, w):
    n = x2.shape[0]
    widths = (PA_W, PB_W, PC_W, PD_W)
    return pl.pallas_call(
        _inproj_kernel, grid=(n // ROW_TILE,),
        in_specs=[pl.BlockSpec((ROW_TILE, D_MODEL), lambda i: (i, 0)),
                  _const_spec((1, D_MODEL)), _const_spec(w.shape)],
        out_specs=[pl.BlockSpec((ROW_TILE, wd), lambda i: (i, 0)) for wd in widths],
        out_shape=[jax.ShapeDtypeStruct((n, wd), F32) for wd in widths],
        compiler_params=_params(("parallel",)), name="inproj")(x2, g, w)


def _outproj_kernel(x_ref, oa_ref, ob_ref, oc_ref, od_ref, w_ref, o_ref):
    acc = x_ref[...]
    for j, ref in enumerate((oa_ref, ob_ref, oc_ref, od_ref)):
        acc = acc + _dot(ref[...], w_ref[j * GROUP_WIDTH:(j + 1) * GROUP_WIDTH, :])
    o_ref[...] = acc


def _outproj(x2, outs, w):
    n = x2.shape[0]
    row = pl.BlockSpec((ROW_TILE, D_MODEL), lambda i: (i, 0))
    grp = pl.BlockSpec((ROW_TILE, GROUP_WIDTH), lambda i: (i, 0))
    return pl.pallas_call(
        _outproj_kernel, grid=(n // ROW_TILE,),
        in_specs=[row, grp, grp, grp, grp, _const_spec(w.shape)], out_specs=row,
        out_shape=jax.ShapeDtypeStruct(x2.shape, F32),
        compiler_params=_params(("parallel",)), name="outproj")(x2, *outs, w)


def _memkv_kernel(m_ref, g_ref, w_ref, kv_ref):
    h = _rms(m_ref[...], g_ref[...]).astype(BF16)
    kv_ref[...] = _dot(h, w_ref[...]).astype(BF16)


def _memkv(mem2, g, w):
    n = mem2.shape[0]
    return pl.pallas_call(
        _memkv_kernel, grid=(n // ROW_TILE,),
        in_specs=[pl.BlockSpec((ROW_TILE, D_MODEL), lambda i: (i, 0)),
                  _const_spec((1, D_MODEL)), _const_spec(w.shape)],
        out_specs=pl.BlockSpec((ROW_TILE, 2 * D_MODEL), lambda i: (i, 0)),
        out_shape=jax.ShapeDtypeStruct((n, 2 * D_MODEL), BF16),
        compiler_params=_params(("parallel",)), name="memkv")(mem2, g, w)


def _cross_kernel(x_ref, g_ref, wq_ref, kv_ref, wo_ref, o_ref):
    x = x_ref[...]
    h = _rms(x, g_ref[...]).astype(BF16)
    q = _dot(h, wq_ref[...]).astype(BF16)
    heads = []
    for hd in range(HEADS):
        lo = hd * X_HD
        s = _dot_t(q[:, lo:lo + X_HD], kv_ref[:, lo:lo + X_HD]) * (X_HD ** -0.5)
        s = s - jnp.max(s, axis=-1, keepdims=True)
        p = jnp.exp(s)
        p = p / jnp.sum(p, axis=-1, keepdims=True)
        heads.append(_dot(p.astype(BF16), kv_ref[:, D_MODEL + lo:D_MODEL + lo + X_HD]).astype(BF16))
    o = jnp.concatenate(heads, axis=-1)
    o_ref[...] = x + _dot(o, wo_ref[...])


def _cross(x2, g, wq, kv, wo, seq, mem_len):
    n = x2.shape[0]
    tiles_per_seq = seq // ROW_TILE
    row = pl.BlockSpec((ROW_TILE, D_MODEL), lambda i: (i, 0))
    return pl.pallas_call(
        _cross_kernel, grid=(n // ROW_TILE,),
        in_specs=[row, _const_spec((1, D_MODEL)), _const_spec(wq.shape),
                  pl.BlockSpec((mem_len, 2 * D_MODEL), lambda i: (i // tiles_per_seq, 0)),
                  _const_spec(wo.shape)],
        out_specs=row, out_shape=jax.ShapeDtypeStruct(x2.shape, F32),
        compiler_params=_params(("parallel",)), name="cross")(x2, g, wq, kv, wo)


def _gla_constants(dk):
    c, h = GLA_CHUNK, HEADS
    w, wv = h * dk, h * HEAD_DV
    r = np.arange(c)
    blocks = [(r[:, None] >= r[None, :])]
    masks = [(r[:, None] == r[None, :])]
    m = 1
    while m < c:
        ref = (r // (2 * m)) * (2 * m) + m - 1
        upper = (r % (2 * m)) >= m
        rp = r[None, :]
        rng_up = (rp > ref[:, None]) & (rp <= r[:, None])
        rng_lo = (rp > r[:, None]) & (rp <= ref[:, None])
        blocks.append(np.where(upper[:, None], rng_up, rng_lo))
        same = (r[:, None] // (2 * m)) == (r[None, :] // (2 * m))
        masks.append(same & upper[:, None] & (~upper)[None, :])
        m *= 2
    nmat = np.concatenate(blocks, axis=0).astype(np.float32)
    lvl = np.stack([np.tile(mk, (1, h)) for mk in masks]).astype(np.float32)
    rows = np.arange(h * c)[:, None] // c
    hm_k = (rows == (np.arange(w)[None, :] // dk)).astype(np.float32)
    hm_v = (rows == (np.arange(wv)[None, :] // HEAD_DV)).astype(np.float32)
    bm_t = ((np.arange(wv)[:, None] // HEAD_DV) == (np.arange(w)[None, :] // dk)).astype(np.float32)
    ones_blk = ((np.arange(wv)[:, None] // HEAD_DV) == (np.arange(wv)[None, :] // HEAD_DV))
    return (jnp.asarray(nmat, BF16), jnp.asarray(lvl, F32), jnp.asarray(hm_k, BF16),
            jnp.asarray(hm_v, BF16), jnp.asarray(bm_t, F32), jnp.asarray(ones_blk.astype(np.float32), BF16))


def _gla_recurrence(seq, q_ref, k_ref, v_ref, g_ref, o_ref, st_ref,
                    nmat_ref, lvl_ref, hmk_ref, hmv_ref, bmt_ref):
    c, h = GLA_CHUNK, HEADS
    n_lvl = lvl_ref.shape[0]
    st_ref[...] = jnp.zeros_like(st_ref)

    def step(ci, carry):
        r0 = pl.multiple_of(ci * c, c)
        q = q_ref[pl.ds(r0, c), :]
        k = k_ref[pl.ds(r0, c), :]
        v = v_ref[pl.ds(r0, c), :]
        g = g_ref[pl.ds(r0, c), :]
        g_hi, g_mid, g_lo = _split3(g)
        cum = nmat_ref[0:c, :]
        b = _dot(cum, g_hi) + _dot(cum, g_mid) + _dot(cum, g_lo)
        lvl_rows = nmat_ref[c:, :]
        sums = _dot(lvl_rows, g_hi) + _dot(lvl_rows, g_mid)
        hmk = hmk_ref[...]
        acc = jnp.zeros((c, h * c), F32)
        for li in range(n_lvl):
            if li == 0:
                qt, kt = q, k
            else:
                e = jnp.exp(sums[(li - 1) * c:li * c])
                qt, kt = q * e, k * e
            kst = jnp.concatenate([kt.astype(BF16)] * h, axis=0) * hmk
            acc = acc + _dot_t(qt.astype(BF16), kst) * lvl_ref[li]
        vb = v.astype(BF16)
        vst = jnp.concatenate([vb] * h, axis=0) * hmv_ref[...]
        o = _dot(acc.astype(BF16), vst)
        st = st_ref[...]
        o = o + _dot_t((q * jnp.exp(b)).astype(BF16), st.astype(BF16))
        b_last = b[c - 1:c, :]
        kh = (k * jnp.exp(b_last - b)).astype(BF16)
        st_ref[...] = st * jnp.exp(b_last) + _tdot(vb, kh) * bmt_ref[...]
        o_ref[pl.ds(r0, c), :] = o
        return carry

    lax.fori_loop(0, seq // c, step, 0, unroll=GLA_UNROLL)


def _head_rms_gate(o, gain, gate, ones_blk):
    sq = o * o
    hi = sq.astype(BF16)
    lo = (sq - hi.astype(F32)).astype(BF16)
    ms = (_dot(hi, ones_blk) + _dot(lo, ones_blk)) * (1.0 / HEAD_DV)
    return o * lax.rsqrt(ms + EPS) * gain * (gate * jax.nn.sigmoid(gate))


def _hgrn_kernel(p_ref, lb_ref, gain_ref, nmat_ref, lvl_ref, hmk_ref, hmv_ref, bmt_ref, ones_ref,
                 o_ref, q_s, k_s, g_s, o_s, st_s):
    seq, gw = o_ref.shape
    z = p_ref[:, gw:2 * gw]
    lb = lb_ref[...]
    lbf = jnp.maximum(lb, HG_LB_FLOOR)
    e = jnp.exp(-jnp.abs(z))
    inv = 1.0 / (1.0 + e)
    sig = jnp.where(z >= 0, inv, e * inv)
    nsig = jnp.where(z >= 0, e * inv, inv)
    f = lbf + (1.0 - lb) * sig
    g_s[...] = jnp.log(f)
    k_s[...] = (1.0 - lb) * nsig - (lbf - lb)
    q_s[...] = p_ref[:, 0:gw]
    _gla_recurrence(seq, q_s, k_s, p_ref.at[:, 2 * gw:3 * gw], g_s, o_s, st_s,
                    nmat_ref, lvl_ref, hmk_ref, hmv_ref, bmt_ref)
    o_ref[...] = _head_rms_gate(o_s[...], gain_ref[...], p_ref[:, 3 * gw:4 * gw],
                                ones_ref[...]).astype(o_ref.dtype)


def _gla_kernel(p_ref, gw_ref, gb_ref, gain_ref, nmat_ref, lvl_ref, hmk_ref, hmv_ref, bmt_ref,
                ones_ref, o_ref, q_s, g_s, o_s, st_s):
    seq, gw = o_ref.shape
    wk = HEADS * GLA_DK
    lr = p_ref[:, 2 * wk + 2 * gw:2 * wk + 2 * gw + LANES]
    lr_hi = lr.astype(BF16)
    lr_lo = (lr - lr_hi.astype(F32)).astype(BF16)
    w = gw_ref[...]
    w_hi = w.astype(BF16)
    w_lo = (w - w_hi.astype(F32)).astype(BF16)
    y = _dot(lr_hi, w_hi) + _dot(lr_hi, w_lo) + _dot(lr_lo, w_hi) + gb_ref[...]
    g_s[...] = (jnp.minimum(y, 0.0) - jnp.log(1.0 + jnp.exp(-jnp.abs(y)))) * (1.0 / GLA_TAU)
    q_s[...] = p_ref[:, 0:wk] * (GLA_DK ** -0.5)
    _gla_recurrence(seq, q_s, p_ref.at[:, wk:2 * wk], p_ref.at[:, 2 * wk:2 * wk + gw], g_s, o_s, st_s,
                    nmat_ref, lvl_ref, hmk_ref, hmv_ref, bmt_ref)
    o_ref[...] = _head_rms_gate(o_s[...], gain_ref[...], p_ref[:, 2 * wk + gw:2 * wk + 2 * gw],
                                ones_ref[...]).astype(o_ref.dtype)


def _seq_spec(seq, width):
    return pl.BlockSpec((seq, width), lambda b: (b, 0))


def _hgrn(pa, lb, gain, batch, seq):
    consts = _gla_constants(HG_DK)
    w = HEADS * HG_DK
    return pl.pallas_call(
        _hgrn_kernel, grid=(batch,),
        in_specs=[_seq_spec(seq, PA_W), _const_spec((1, w)), _const_spec((1, GROUP_WIDTH))]
                 + [_const_spec(cst.shape) for cst in consts],
        out_specs=_seq_spec(seq, GROUP_WIDTH),
        out_shape=jax.ShapeDtypeStruct((batch * seq, GROUP_WIDTH), BF16),
        scratch_shapes=[pltpu.VMEM((seq, w), F32), pltpu.VMEM((seq, w), F32), pltpu.VMEM((seq, w), F32),
                        pltpu.VMEM((seq, GROUP_WIDTH), F32), pltpu.VMEM((GROUP_WIDTH, w), F32)],
        compiler_params=_params(("parallel",)), name="hgrn")(pa, lb, gain, *consts)


def _gla(pb, gate_w, gate_b, gain, batch, seq):
    consts = _gla_constants(GLA_DK)
    w = HEADS * GLA_DK
    return pl.pallas_call(
        _gla_kernel, grid=(batch,),
        in_specs=[_seq_spec(seq, PB_W), _const_spec((LANES, w)), _const_spec((1, w)),
                  _const_spec((1, GROUP_WIDTH))] + [_const_spec(cst.shape) for cst in consts],
        out_specs=_seq_spec(seq, GROUP_WIDTH),
        out_shape=jax.ShapeDtypeStruct((batch * seq, GROUP_WIDTH), BF16),
        scratch_shapes=[pltpu.VMEM((seq, w), F32), pltpu.VMEM((seq, w), F32),
                        pltpu.VMEM((seq, GROUP_WIDTH), F32), pltpu.VMEM((GROUP_WIDTH, w), F32)],
        compiler_params=_params(("parallel",)), name="gla")(pb, gate_w, gate_b, gain, *consts)


def _dilated_kernel(p_ref, cos_ref, sin_ref, o_ref, q_s, k_s, v_s, op_s, lse_s):
    seq, gw = o_ref.shape
    n_half = gw // LANES
    heads_per_half = LANES // DA_HD
    lane = lax.broadcasted_iota(jnp.int32, (1, LANES), 1)
    in_head = lane & (DA_HD - 1)
    first_half = in_head < (DA_ROT // 2)
    cos, sin = cos_ref[...], sin_ref[...]

    def rope(t):
        partner = jnp.where(first_half, pltpu.roll(t, LANES - DA_ROT // 2, 1),
                            pltpu.roll(t, DA_ROT // 2, 1))
        return t * cos + partner * sin

    for hf in range(n_half):
        lo = hf * LANES
        q_s[hf] = rope(p_ref[:, lo:lo + LANES]) * (DA_HD ** -0.5)
        k_s[hf] = rope(p_ref[:, gw + lo:gw + lo + LANES])
        v_s[hf] = p_ref[:, 2 * gw + lo:2 * gw + lo + LANES]

    nq = DA_STEPS
    qi = lax.broadcasted_iota(jnp.int32, (nq, 2 * nq), 0)
    kj = lax.broadcasted_iota(jnp.int32, (nq, 2 * nq), 1)
    cur_ok = (kj >= nq) & (kj - nq <= qi)
    head_of_lane = lane >> int(math.log2(DA_HD))

    for pi, (window, dil) in enumerate(DA_PATTERNS):
        n_blk = seq // (dil * nq)

        def block(i, carry, pi=pi, dil=dil, n_blk=n_blk):
            r = i // n_blk
            n = i % n_blk
            cur0 = r + n * (nq * dil)
            prev0 = r + jnp.maximum(n - 1, 0) * (nq * dil)

            def rows(start):
                if dil == 1:
                    return pl.ds(pl.multiple_of(start, nq), nq)
                return pl.ds(start, nq, stride=dil)

            first_row = qi + jnp.where(n > 0, 0, nq)
            valid = cur_ok | ((kj < nq) & (kj >= first_row))
            for hf in range(n_half):
                q = q_s[hf, rows(cur0), :]
                kwin = jnp.concatenate([k_s[hf, rows(prev0), :], k_s[hf, rows(cur0), :]], axis=0).astype(BF16)
                vwin = jnp.concatenate([v_s[hf, rows(prev0), :], v_s[hf, rows(cur0), :]], axis=0).astype(BF16)
                o_acc = jnp.zeros((nq, LANES), F32)
                lse_acc = jnp.zeros((nq, LANES), F32)
                for hd in range(heads_per_half):
                    hm = head_of_lane == hd
                    s = _dot_t(jnp.where(hm, q, 0.0).astype(BF16), kwin)
                    s = jnp.where(valid, s, MASK_VALUE)
                    m = jnp.max(s, axis=-1, keepdims=True)
                    p = jnp.exp(s - m)
                    l = jnp.sum(p, axis=-1, keepdims=True)
                    oh = _dot(p.astype(BF16), vwin) / l
                    o_acc = jnp.where(hm, oh, o_acc)
                    lse_acc = jnp.where(hm, m + jnp.log(l), lse_acc)
                op_s[pi, hf, rows(cur0), :] = o_acc
                lse_s[pi, hf, rows(cur0), :] = lse_acc
            return carry

        lax.fori_loop(0, dil * n_blk, block, 0, unroll=DA_UNROLL)

    for hf in range(n_half):
        lses = [lse_s[pi, hf] for pi in range(len(DA_PATTERNS))]
        mx = functools.reduce(jnp.maximum, lses)
        ws = [jnp.exp(l - mx) for l in lses]
        num = sum(wt * op_s[pi, hf] for pi, wt in enumerate(ws))
        o_ref[:, hf * LANES:(hf + 1) * LANES] = (num / sum(ws)).astype(o_ref.dtype)


def _dilated(pc, cos_t, sin_t, batch, seq):
    gw = GROUP_WIDTH
    n_pat = len(DA_PATTERNS)
    n_half = gw // LANES
    return pl.pallas_call(
        _dilated_kernel, grid=(batch,),
        in_specs=[_seq_spec(seq, PC_W), _seq_spec(seq, LANES), _seq_spec(seq, LANES)],
        out_specs=_seq_spec(seq, gw),
        out_shape=jax.ShapeDtypeStruct((batch * seq, gw), BF16),
        scratch_shapes=[pltpu.VMEM((n_half, seq, LANES), F32)] * 3
                       + [pltpu.VMEM((n_pat, n_half, seq, LANES), F32)] * 2,
        compiler_params=_params(("parallel",)), name="dilated")(pc, cos_t, sin_t)


def _conv_kernel(p_ref, w_ref, b_ref, lg_ref, lb_ref, o_ref, u_s):
    seq, gw = o_ref.shape
    a = p_ref[:, 0:gw]
    gate = p_ref[:, gw:2 * gw]
    u_s[0:CONV_PAD, :] = jnp.zeros((CONV_PAD, gw), F32)
    u_s[CONV_PAD:CONV_PAD + seq, :] = a * jax.nn.sigmoid(gate)
    w = w_ref[...]
    first = CONV_PAD - (CONV_K - 1)
    for t in range(seq // CONV_TILE):
        t0 = t * CONV_TILE
        y = jnp.zeros((CONV_TILE, gw), F32) + b_ref[...]
        for j in range(CONV_K):
            y = y + u_s[t0 + first + j:t0 + first + j + CONV_TILE, :] * w[j:j + 1, :]
        mu = jnp.mean(y, axis=-1, keepdims=True)
        d = y - mu
        var = jnp.mean(d * d, axis=-1, keepdims=True)
        yn = d * lax.rsqrt(var + EPS) * lg_ref[...] + lb_ref[...]
        o_ref[t0:t0 + CONV_TILE, :] = (yn * jax.nn.sigmoid(yn)).astype(o_ref.dtype)


def _conv(pd, w, b, ln_g, ln_b, batch, seq):
    gw = GROUP_WIDTH
    vec = _const_spec((1, gw))
    return pl.pallas_call(
        _conv_kernel, grid=(batch,),
        in_specs=[_seq_spec(seq, PD_W), _const_spec(w.shape), vec, vec, vec],
        out_specs=_seq_spec(seq, gw),
        out_shape=jax.ShapeDtypeStruct((batch * seq, gw), BF16),
        scratch_shapes=[pltpu.VMEM((CONV_PAD + seq, gw), F32)],
        compiler_params=_params(("parallel",)), name="conv")(pd, w, b, ln_g, ln_b)


def _rope_tables(positions):
    half = DA_ROT // 2
    inv_freq = jnp.power(jnp.float32(ROPE_THETA), -jnp.arange(0, DA_ROT, 2, dtype=F32) / DA_ROT)
    ang = positions.astype(F32)[..., None] * inv_freq
    cos, sin = jnp.cos(ang), jnp.sin(ang)
    rest = DA_HD - DA_ROT
    ones = jnp.ones(cos.shape[:-1] + (rest,), F32)
    cos_h = jnp.concatenate([cos, cos, ones], axis=-1)
    sin_h = jnp.concatenate([-sin, sin, 0.0 * ones], axis=-1)
    b, s = positions.shape
    tile = lambda t: jnp.tile(t, (1, 1, LANES // DA_HD)).reshape(b * s, LANES)
    return tile(cos_h), tile(sin_h)


def _relayout_w_in(w_in):
    a_end = PA_W
    wk = HEADS * GLA_DK
    b_qkv = w_in[:, a_end:a_end + 2 * wk + GROUP_WIDTH]
    lr0 = a_end + 2 * wk + GROUP_WIDTH
    b_lr = w_in[:, lr0:lr0 + GLA_RANK]
    b_r = w_in[:, lr0 + GLA_RANK:lr0 + GLA_RANK + GROUP_WIDTH]
    c0 = lr0 + GLA_RANK + GROUP_WIDTH
    rest = w_in[:, c0:]
    pad = jnp.zeros((w_in.shape[0], LANES - GLA_RANK), w_in.dtype)
    return jnp.concatenate([w_in[:, :a_end], b_qkv, b_r, b_lr, pad, rest], axis=1).astype(BF16)


def kernel(x, mem, positions, hgrn_lb_logits, ffn1_norm, ffn1_w_up, ffn1_w_down, mix_norm, w_in, hgrn_out_norm, gla_gate_w, gla_gate_b, gla_out_norm, conv_w, conv_b, conv_ln_g, conv_ln_b, w_out, cross_norm, mem_norm, cross_wq, cross_wkv, cross_wo, ffn2_norm, ffn2_w_up, ffn2_w_down, final_norm):
    batch, seq, d = x.shape
    mem_len = mem.shape[1]
    depth = w_in.shape[0]
    assert d == D_MODEL and seq % (DA_PATTERNS[-1][1] * DA_STEPS) == 0
    assert (batch * seq) % ROW_TILE == 0 and seq % ROW_TILE == 0 and (batch * mem_len) % ROW_TILE == 0

    cos_t, sin_t = _rope_tables(positions)
    p_lb = jax.nn.softmax(hgrn_lb_logits.astype(F32), axis=0)
    lower_bounds = jnp.cumsum(p_lb, axis=0) - p_lb[0:1]

    row = lambda v: v.reshape(1, -1).astype(F32)
    x2 = x.reshape(batch * seq, d)
    mem2 = mem.reshape(batch * mem_len, d)
    for l in range(depth):
        x2 = _ffn(x2, row(ffn1_norm[l]), ffn1_w_up[l].astype(BF16), ffn1_w_down[l].astype(BF16))

        pa, pb, pc, pd = _inproj(x2, row(mix_norm[l]), _relayout_w_in(w_in[l]))
        gate_w = jnp.zeros((LANES, HEADS * GLA_DK), F32).at[:GLA_RANK].set(gla_gate_w[l])
        o_a = _hgrn(pa, row(lower_bounds[l]), row(hgrn_out_norm[l]), batch, seq)
        o_b = _gla(pb, gate_w, row(gla_gate_b[l]), row(gla_out_norm[l]), batch, seq)
        o_c = _dilated(pc, cos_t, sin_t, batch, seq)
        o_d = _conv(pd, conv_w[l], row(conv_b[l]), row(conv_ln_g[l]), row(conv_ln_b[l]), batch, seq)
        x2 = _outproj(x2, (o_a, o_b, o_c, o_d), w_out[l].astype(BF16))

        kv = _memkv(mem2, row(mem_norm[l]), cross_wkv[l].astype(BF16))
        x2 = _cross(x2, row(cross_norm[l]), cross_wq[l].astype(BF16), kv, cross_wo[l].astype(BF16),
                    seq, mem_len)

        last = l == depth - 1
        x2 = _ffn(x2, row(ffn2_norm[l]), ffn2_w_up[l].astype(BF16), ffn2_w_down[l].astype(BF16),
                  final_g=row(final_norm) if last else None)
    return x2.reshape(batch, seq, d)
```

```python
import functools
import math

import numpy as np
import jax
import jax.numpy as jnp
from jax import lax
from jax.experimental import pallas as pl
from jax.experimental.pallas import tpu as pltpu

F32 = jnp.float32
BF16 = jnp.bfloat16

D_MODEL = 1024
GROUP_WIDTH = D_MODEL // 4
HEADS = 4
HG_DK = GROUP_WIDTH // HEADS
GLA_DK = GROUP_WIDTH // (2 * HEADS)
HEAD_DV = GROUP_WIDTH // HEADS
GLA_RANK = 16
GLA_TAU = 16.0
HG_LB_FLOOR = 1e-20
DA_HD = GROUP_WIDTH // HEADS
DA_ROT = DA_HD // 4
ROPE_THETA = 500000.0
DA_PATTERNS = ((128, 1), (512, 4), (2048, 16))
MASK_VALUE = -1e30
CONV_K = 31
X_HD = D_MODEL // HEADS
D_FF = ((int(8 * D_MODEL / 3) + 255) // 256) * 256
EPS = 1e-6

LANES = 128
SUBLANES = 8
ROW_TILE = 512
FFN_TILE = 1024
FF_CHUNK = D_FF // 4
GLA_CHUNK = 64
GLA_UNROLL = 4
DA_STEPS = 128
DA_UNROLL = 4
CONV_TILE = 128
CONV_PAD = 32
VMEM_LIMIT = 56 * 1024 * 1024

PA_W = 4 * GROUP_WIDTH
PB_W = 2 * HEADS * GLA_DK + 2 * GROUP_WIDTH + LANES
PC_W = 3 * GROUP_WIDTH
PD_W = 2 * GROUP_WIDTH


def _dot(a, b):
    return jnp.dot(a, b, preferred_element_type=F32)


def _dot_t(a, b):
    return lax.dot_general(a, b, (((1,), (1,)), ((), ())), preferred_element_type=F32)


def _tdot(a, b):
    return lax.dot_general(a, b, (((0,), (0,)), ((), ())), preferred_element_type=F32)


def _rms(x, g):
    ms = jnp.mean(x * x, axis=-1, keepdims=True)
    return x * lax.rsqrt(ms + EPS) * g


def _split3(x):
    hi = x.astype(BF16)
    r1 = x - hi.astype(F32)
    mid = r1.astype(BF16)
    lo = (r1 - mid.astype(F32)).astype(BF16)
    return hi, mid, lo


def _const_spec(shape):
    nd = len(shape)
    return pl.BlockSpec(shape, lambda *_: (0,) * nd, pipeline_mode=pl.Buffered(1))


def _layer_spec(stacked, layer):
    nd = stacked.ndim - 1
    return pl.BlockSpec((None,) + stacked.shape[1:], lambda *_: (layer,) + (0,) * nd,
                        pipeline_mode=pl.Buffered(1))


def _params(sem):
    return pltpu.CompilerParams(dimension_semantics=sem, vmem_limit_bytes=VMEM_LIMIT)


def _ffn_body(x_ref, g_ref, wup_ref, wd_ref):
    x = x_ref[...]
    h = _rms(x, g_ref[...]).astype(BF16)
    acc = jnp.zeros_like(x)
    for c in range(D_FF // FF_CHUNK):
        lo = c * FF_CHUNK
        gate = _dot(h, wup_ref[:, lo:lo + FF_CHUNK])
        up = _dot(h, wup_ref[:, D_FF + lo:D_FF + lo + FF_CHUNK])
        a = (gate * jax.nn.sigmoid(gate) * up).astype(BF16)
        acc = acc + _dot(a, wd_ref[lo:lo + FF_CHUNK, :])
    return x + 0.5 * acc


def _ffn_kernel(x_ref, g_ref, wup_ref, wd_ref, o_ref):
    o_ref[...] = _ffn_body(x_ref, g_ref, wup_ref, wd_ref)


def _ffn_final_kernel(x_ref, g_ref, wup_ref, wd_ref, fg_ref, o_ref):
    o_ref[...] = _rms(_ffn_body(x_ref, g_ref, wup_ref, wd_ref), fg_ref[...])


def _ffn(x2, g, wup, wd, layer, final_g=None):
    n = x2.shape[0]
    row = pl.BlockSpec((FFN_TILE, D_MODEL), lambda i: (i, 0))
    in_specs = [row, _const_spec((1, D_MODEL)), _layer_spec(wup, layer), _layer_spec(wd, layer)]
    args = [x2, g, wup, wd]
    body = _ffn_kernel
    if final_g is not None:
        in_specs.append(_const_spec((1, D_MODEL)))
        args.append(final_g)
        body = _ffn_final_kernel
    return pl.pallas_call(
        body, grid=(n // FFN_TILE,), in_specs=in_specs, out_specs=row,
        out_shape=jax.ShapeDtypeStruct(x2.shape, F32),
        compiler_params=_params(("parallel",)), name="ffn")(*args)


def _inproj_kernel(x_ref, g_ref, w_ref, pa_ref, pb_ref, pc_ref, pd_ref):
    h = _rms(x_ref[...], g_ref[...]).astype(BF16)
    off = 0
    for ref, width in ((pa_ref, PA_W), (pb_ref, PB_W), (pc_ref, PC_W), (pd_ref, PD_W)):
        ref[...] = _dot(h, w_ref[:, off:off + width])
        off += width


def _inproj(x2, g, w, layer):
    n = x2.shape[0]
    widths = (PA_W, PB_W, PC_W, PD_W)
    return pl.pallas_call(
        _inproj_kernel, grid=(n // ROW_TILE,),
        in_specs=[pl.BlockSpec((ROW_TILE, D_MODEL), lambda i: (i, 0)),
                  _const_spec((1, D_MODEL)), _layer_spec(w, layer)],
        out_specs=[pl.BlockSpec((ROW_TILE, wd), lambda i: (i, 0)) for wd in widths],
        out_shape=[jax.ShapeDtypeStruct((n, wd), F32) for wd in widths],
        compiler_params=_params(("parallel",)), name="inproj")(x2, g, w)


def _memkv_kernel(m_ref, g_ref, w_ref, kv_ref):
    h = _rms(m_ref[...], g_ref[...]).astype(BF16)
    kv_ref[...] = _dot(h, w_ref[...]).astype(BF16)


def _memkv(mem2, g, w, layer):
    n = mem2.shape[0]
    return pl.pallas_call(
        _memkv_kernel, grid=(n // ROW_TILE,),
        in_specs=[pl.BlockSpec((ROW_TILE, D_MODEL), lambda i: (i, 0)),
                  _const_spec((1, D_MODEL)), _layer_spec(w, layer)],
        out_specs=pl.BlockSpec((ROW_TILE, 2 * D_MODEL), lambda i: (i, 0)),
        out_shape=jax.ShapeDtypeStruct((n, 2 * D_MODEL), BF16),
        compiler_params=_params(("parallel",)), name="memkv")(mem2, g, w)


def _cross_kernel(x_ref, oa_ref, ob_ref, oc_ref, od_ref, wout_ref, g_ref, wq_ref, kv_ref, wo_ref, o_ref):
    x = x_ref[...]
    for j, ref in enumerate((oa_ref, ob_ref, oc_ref, od_ref)):
        x = x + _dot(ref[...], wout_ref[j * GROUP_WIDTH:(j + 1) * GROUP_WIDTH, :])
    h = _rms(x, g_ref[...]).astype(BF16)
    q = _dot(h, wq_ref[...]).astype(BF16)
    heads = []
    for hd in range(HEADS):
        lo = hd * X_HD
        s = _dot_t(q[:, lo:lo + X_HD], kv_ref[:, lo:lo + X_HD]) * (X_HD ** -0.5)
        s = s - jnp.max(s, axis=-1, keepdims=True)
        p = jnp.exp(s)
        p = p / jnp.sum(p, axis=-1, keepdims=True)
        heads.append(_dot(p.astype(BF16), kv_ref[:, D_MODEL + lo:D_MODEL + lo + X_HD]).astype(BF16))
    o = jnp.concatenate(heads, axis=-1)
    o_ref[...] = x + _dot(o, wo_ref[...])


def _cross(x2, outs, w_out, g, wq, kv, wo, layer, seq, mem_len):
    n = x2.shape[0]
    tiles_per_seq = seq // ROW_TILE
    row = pl.BlockSpec((ROW_TILE, D_MODEL), lambda i: (i, 0))
    grp = pl.BlockSpec((ROW_TILE, GROUP_WIDTH), lambda i: (i, 0))
    return pl.pallas_call(
        _cross_kernel, grid=(n // ROW_TILE,),
        in_specs=[row, grp, grp, grp, grp, _layer_spec(w_out, layer),
                  _const_spec((1, D_MODEL)), _layer_spec(wq, layer),
                  pl.BlockSpec((mem_len, 2 * D_MODEL), lambda i: (i // tiles_per_seq, 0)),
                  _layer_spec(wo, layer)],
        out_specs=row, out_shape=jax.ShapeDtypeStruct(x2.shape, F32),
        compiler_params=_params(("parallel",)), name="cross")(x2, *outs, w_out, g, wq, kv, wo)


def _gla_constants(dk):
    c, h = GLA_CHUNK, HEADS
    w, wv = h * dk, h * HEAD_DV
    r = np.arange(c)
    blocks = [(r[:, None] >= r[None, :])]
    masks = [(r[:, None] == r[None, :])]
    m = 1
    while m < c:
        ref = (r // (2 * m)) * (2 * m) + m - 1
        upper = (r % (2 * m)) >= m
        rp = r[None, :]
        rng_up = (rp > ref[:, None]) & (rp <= r[:, None])
        rng_lo = (rp > r[:, None]) & (rp <= ref[:, None])
        blocks.append(np.where(upper[:, None], rng_up, rng_lo))
        same = (r[:, None] // (2 * m)) == (r[None, :] // (2 * m))
        masks.append(same & upper[:, None] & (~upper)[None, :])
        m *= 2
    nmat = np.concatenate(blocks, axis=0).astype(np.float32)
    lvl = np.stack([np.tile(mk, (1, h)) for mk in masks]).astype(np.float32)
    rows = np.arange(h * c)[:, None] // c
    hm_k = (rows == (np.arange(w)[None, :] // dk)).astype(np.float32)
    hm_v = (rows == (np.arange(wv)[None, :] // HEAD_DV)).astype(np.float32)
    bm_t = ((np.arange(wv)[:, None] // HEAD_DV) == (np.arange(w)[None, :] // dk)).astype(np.float32)
    ones_blk = ((np.arange(wv)[:, None] // HEAD_DV) == (np.arange(wv)[None, :] // HEAD_DV))
    return (jnp.asarray(nmat, BF16), jnp.asarray(lvl, F32), jnp.asarray(hm_k, BF16),
            jnp.asarray(hm_v, BF16), jnp.asarray(bm_t, F32), jnp.asarray(ones_blk.astype(np.float32), BF16))


def _gla_recurrence(seq, q_ref, k_ref, v_ref, g_ref, o_ref, st_ref,
                    nmat_ref, lvl_ref, hmk_ref, hmv_ref, bmt_ref):
    c, h = GLA_CHUNK, HEADS
    n_lvl = lvl_ref.shape[0]
    st_ref[...] = jnp.zeros_like(st_ref)

    def step(ci, carry):
        r0 = pl.multiple_of(ci * c, c)
        q = q_ref[pl.ds(r0, c), :]
        k = k_ref[pl.ds(r0, c), :]
        v = v_ref[pl.ds(r0, c), :]
        g = g_ref[pl.ds(r0, c), :]
        g_hi = g.astype(BF16)
        g_mid = (g - g_hi.astype(F32)).astype(BF16)
        cum = nmat_ref[0:c, :]
        b = _dot(cum, g_hi) + _dot(cum, g_mid)
        lvl_rows = nmat_ref[c:, :]
        sums = _dot(lvl_rows, g_hi)
        hmk = hmk_ref[...]
        acc = jnp.zeros((c, h * c), F32)
        for li in range(n_lvl):
            if li == 0:
                qt, kt = q, k
            else:
                e = jnp.exp(sums[(li - 1) * c:li * c])
                qt, kt = q * e, k * e
            kst = jnp.concatenate([kt.astype(BF16)] * h, axis=0) * hmk
            acc = acc + _dot_t(qt.astype(BF16), kst) * lvl_ref[li]
        vb = v.astype(BF16)
        vst = jnp.concatenate([vb] * h, axis=0) * hmv_ref[...]
        o = _dot(acc.astype(BF16), vst)
        st = st_ref[...]
        o = o + _dot_t((q * jnp.exp(b)).astype(BF16), st.astype(BF16))
        b_last = b[c - 1:c, :]
        kh = (k * jnp.exp(b_last - b)).astype(BF16)
        st_ref[...] = st * jnp.exp(b_last) + _tdot(vb, kh) * bmt_ref[...]
        o_ref[pl.ds(r0, c), :] = o
        return carry

    lax.fori_loop(0, seq // c, step, 0, unroll=GLA_UNROLL)


def _head_rms_gate(o, gain, gate, ones_blk):
    sq = o * o
    hi = sq.astype(BF16)
    lo = (sq - hi.astype(F32)).astype(BF16)
    ms = (_dot(hi, ones_blk) + _dot(lo, ones_blk)) * (1.0 / HEAD_DV)
    return o * lax.rsqrt(ms + EPS) * gain * (gate * jax.nn.sigmoid(gate))


def _hgrn_kernel(p_ref, lb_ref, gain_ref, nmat_ref, lvl_ref, hmk_ref, hmv_ref, bmt_ref, ones_ref,
                 o_ref, q_s, k_s, g_s, o_s, st_s):
    seq, gw = o_ref.shape
    z = p_ref[:, gw:2 * gw]
    lb = lb_ref[...]
    lbf = jnp.maximum(lb, HG_LB_FLOOR)
    e = jnp.exp(-jnp.abs(z))
    inv = 1.0 / (1.0 + e)
    sig = jnp.where(z >= 0, inv, e * inv)
    nsig = jnp.where(z >= 0, e * inv, inv)
    f = lbf + (1.0 - lb) * sig
    g_s[...] = jnp.log(f)
    k_s[...] = (1.0 - lb) * nsig - (lbf - lb)
    q_s[...] = p_ref[:, 0:gw]
    _gla_recurrence(seq, q_s, k_s, p_ref.at[:, 2 * gw:3 * gw], g_s, o_s, st_s,
                    nmat_ref, lvl_ref, hmk_ref, hmv_ref, bmt_ref)
    o_ref[...] = _head_rms_gate(o_s[...], gain_ref[...], p_ref[:, 3 * gw:4 * gw],
                                ones_ref[...]).astype(o_ref.dtype)


def _gla_kernel(p_ref, gw_ref, gb_ref, gain_ref, nmat_ref, lvl_ref, hmk_ref, hmv_ref, bmt_ref,
                ones_ref, o_ref, q_s, g_s, o_s, st_s):
    seq, gw = o_ref.shape
    wk = HEADS * GLA_DK
    lr = p_ref[:, 2 * wk + 2 * gw:2 * wk + 2 * gw + LANES]
    lr_hi = lr.astype(BF16)
    lr_lo = (lr - lr_hi.astype(F32)).astype(BF16)
    w = gw_ref[...]
    w_hi = w.astype(BF16)
    w_lo = (w - w_hi.astype(F32)).astype(BF16)
    y = _dot(lr_hi, w_hi) + _dot(lr_hi, w_lo) + _dot(lr_lo, w_hi) + gb_ref[...]
    g_s[...] = (jnp.minimum(y, 0.0) - jnp.log(1.0 + jnp.exp(-jnp.abs(y)))) * (1.0 / GLA_TAU)
    q_s[...] = p_ref[:, 0:wk] * (GLA_DK ** -0.5)
    _gla_recurrence(seq, q_s, p_ref.at[:, wk:2 * wk], p_ref.at[:, 2 * wk:2 * wk + gw], g_s, o_s, st_s,
                    nmat_ref, lvl_ref, hmk_ref, hmv_ref, bmt_ref)
    o_ref[...] = _head_rms_gate(o_s[...], gain_ref[...], p_ref[:, 2 * wk + gw:2 * wk + 2 * gw],
                                ones_ref[...]).astype(o_ref.dtype)


def _seq_spec(seq, width):
    return pl.BlockSpec((seq, width), lambda b: (b, 0))


def _hgrn(pa, lb, gain, batch, seq):
    consts = _gla_constants(HG_DK)
    w = HEADS * HG_DK
    return pl.pallas_call(
        _hgrn_kernel, grid=(batch,),
        in_specs=[_seq_spec(seq, PA_W), _const_spec((1, w)), _const_spec((1, GROUP_WIDTH))]
                 + [_const_spec(cst.shape) for cst in consts],
        out_specs=_seq_spec(seq, GROUP_WIDTH),
        out_shape=jax.ShapeDtypeStruct((batch * seq, GROUP_WIDTH), BF16),
        scratch_shapes=[pltpu.VMEM((seq, w), F32), pltpu.VMEM((seq, w), F32), pltpu.VMEM((seq, w), F32),
                        pltpu.VMEM((seq, GROUP_WIDTH), F32), pltpu.VMEM((GROUP_WIDTH, w), F32)],
        compiler_params=_params(("parallel",)), name="hgrn")(pa, lb, gain, *consts)


def _gla(pb, gate_w, gate_b, gain, batch, seq):
    consts = _gla_constants(GLA_DK)
    w = HEADS * GLA_DK
    return pl.pallas_call(
        _gla_kernel, grid=(batch,),
        in_specs=[_seq_spec(seq, PB_W), _const_spec((LANES, w)), _const_spec((1, w)),
                  _const_spec((1, GROUP_WIDTH))] + [_const_spec(cst.shape) for cst in consts],
        out_specs=_seq_spec(seq, GROUP_WIDTH),
        out_shape=jax.ShapeDtypeStruct((batch * seq, GROUP_WIDTH), BF16),
        scratch_shapes=[pltpu.VMEM((seq, w), F32), pltpu.VMEM((seq, w), F32),
                        pltpu.VMEM((seq, GROUP_WIDTH), F32), pltpu.VMEM((GROUP_WIDTH, w), F32)],
        compiler_params=_params(("parallel",)), name="gla")(pb, gate_w, gate_b, gain, *consts)


def _dilated_kernel(p_ref, cos_ref, sin_ref, o_ref, q_s, k_s, v_s, op_s, lse_s):
    seq, gw = o_ref.shape
    n_half = gw // LANES
    heads_per_half = LANES // DA_HD
    lane = lax.broadcasted_iota(jnp.int32, (1, LANES), 1)
    in_head = lane & (DA_HD - 1)
    first_half = in_head < (DA_ROT // 2)
    cos, sin = cos_ref[...], sin_ref[...]

    def rope(t):
        partner = jnp.where(first_half, pltpu.roll(t, LANES - DA_ROT // 2, 1),
                            pltpu.roll(t, DA_ROT // 2, 1))
        return t * cos + partner * sin

    for hf in range(n_half):
        lo = hf * LANES
        q_s[hf] = rope(p_ref[:, lo:lo + LANES]) * (DA_HD ** -0.5)
        k_s[hf] = rope(p_ref[:, gw + lo:gw + lo + LANES])
        v_s[hf] = p_ref[:, 2 * gw + lo:2 * gw + lo + LANES]

    nq = DA_STEPS
    qi = lax.broadcasted_iota(jnp.int32, (nq, 2 * nq), 0)
    kj = lax.broadcasted_iota(jnp.int32, (nq, 2 * nq), 1)
    cur_ok = (kj >= nq) & (kj - nq <= qi)
    head_of_lane = lane >> int(math.log2(DA_HD))

    for pi, (window, dil) in enumerate(DA_PATTERNS):
        n_blk = seq // (dil * nq)

        def block(i, carry, pi=pi, dil=dil, n_blk=n_blk):
            r = i // n_blk
            n = i % n_blk
            cur0 = r + n * (nq * dil)
            prev0 = r + jnp.maximum(n - 1, 0) * (nq * dil)

            def rows(start):
                if dil == 1:
                    return pl.ds(pl.multiple_of(start, nq), nq)
                return pl.ds(start, nq, stride=dil)

            first_row = qi + jnp.where(n > 0, 0, nq)
            valid = cur_ok | ((kj < nq) & (kj >= first_row))
            for hf in range(n_half):
                q = q_s[hf, rows(cur0), :]
                kwin = jnp.concatenate([k_s[hf, rows(prev0), :], k_s[hf, rows(cur0), :]], axis=0).astype(BF16)
                vwin = jnp.concatenate([v_s[hf, rows(prev0), :], v_s[hf, rows(cur0), :]], axis=0).astype(BF16)
                o_acc = jnp.zeros((nq, LANES), F32)
                lse_acc = jnp.zeros((nq, LANES), F32)
                for hd in range(heads_per_half):
                    hm = head_of_lane == hd
                    s = _dot_t(jnp.where(hm, q, 0.0).astype(BF16), kwin)
                    s = jnp.where(valid, s, MASK_VALUE)
                    m = jnp.max(s, axis=-1, keepdims=True)
                    p = jnp.exp(s - m)
                    l = jnp.sum(p, axis=-1, keepdims=True)
                    oh = _dot(p.astype(BF16), vwin) / l
                    o_acc = jnp.where(hm, oh, o_acc)
                    lse_acc = jnp.where(hm, m + jnp.log(l), lse_acc)
                op_s[pi, hf, rows(cur0), :] = o_acc
                lse_s[pi, hf, rows(cur0), :] = lse_acc
            return carry

        lax.fori_loop(0, dil * n_blk, block, 0, unroll=DA_UNROLL)

    for hf in range(n_half):
        lses = [lse_s[pi, hf] for pi in range(len(DA_PATTERNS))]
        mx = functools.reduce(jnp.maximum, lses)
        ws = [jnp.exp(l - mx) for l in lses]
        num = sum(wt * op_s[pi, hf] for pi, wt in enumerate(ws))
        o_ref[:, hf * LANES:(hf + 1) * LANES] = (num / sum(ws)).astype(o_ref.dtype)


def _dilated(pc, cos_t, sin_t, batch, seq):
    gw = GROUP_WIDTH
    n_pat = len(DA_PATTERNS)
    n_half = gw // LANES
    return pl.pallas_call(
        _dilated_kernel, grid=(batch,),
        in_specs=[_seq_spec(seq, PC_W), _seq_spec(seq, LANES), _seq_spec(seq, LANES)],
        out_specs=_seq_spec(seq, gw),
        out_shape=jax.ShapeDtypeStruct((batch * seq, gw), BF16),
        scratch_shapes=[pltpu.VMEM((n_half, seq, LANES), F32)] * 3
                       + [pltpu.VMEM((n_pat, n_half, seq, LANES), F32)] * 2,
        compiler_params=_params(("parallel",)), name="dilated")(pc, cos_t, sin_t)


def _conv_kernel(p_ref, w_ref, b_ref, lg_ref, lb_ref, o_ref, u_s):
    seq, gw = o_ref.shape
    a = p_ref[:, 0:gw]
    gate = p_ref[:, gw:2 * gw]
    u_s[0:CONV_PAD, :] = jnp.zeros((CONV_PAD, gw), F32)
    u_s[CONV_PAD:CONV_PAD + seq, :] = a * jax.nn.sigmoid(gate)
    u_s[CONV_PAD + seq:CONV_PAD + seq + SUBLANES, :] = jnp.zeros((SUBLANES, gw), F32)
    w = w_ref[...]
    first = CONV_PAD - (CONV_K - 1)
    ext = CONV_TILE + SUBLANES

    def tile(ti, carry):
        t0 = pl.multiple_of(ti * CONV_TILE, CONV_TILE)
        y = jnp.zeros((CONV_TILE, gw), F32) + b_ref[...]
        for s in range(SUBLANES):
            z = None
            for a in range(s, first + CONV_K, SUBLANES):
                if a >= first:
                    term = u_s[pl.ds(pl.multiple_of(t0 + (a - s), SUBLANES), ext), :] * w[a - first:a - first + 1, :]
                    z = term if z is None else z + term
            y = y + z[s:s + CONV_TILE, :]
        mu = jnp.mean(y, axis=-1, keepdims=True)
        d = y - mu
        var = jnp.mean(d * d, axis=-1, keepdims=True)
        yn = d * lax.rsqrt(var + EPS) * lg_ref[...] + lb_ref[...]
        o_ref[pl.ds(t0, CONV_TILE), :] = (yn * jax.nn.sigmoid(yn)).astype(o_ref.dtype)
        return carry

    lax.fori_loop(0, seq // CONV_TILE, tile, 0)


def _conv(pd, w, b, ln_g, ln_b, batch, seq):
    gw = GROUP_WIDTH
    vec = _const_spec((1, gw))
    return pl.pallas_call(
        _conv_kernel, grid=(batch,),
        in_specs=[_seq_spec(seq, PD_W), _const_spec(w.shape), vec, vec, vec],
        out_specs=_seq_spec(seq, gw),
        out_shape=jax.ShapeDtypeStruct((batch * seq, gw), BF16),
        scratch_shapes=[pltpu.VMEM((CONV_PAD + seq + SUBLANES, gw), F32)],
        compiler_params=_params(("parallel",)), name="conv")(pd, w, b, ln_g, ln_b)


def _rope_tables(positions):
    half = DA_ROT // 2
    inv_freq = jnp.power(jnp.float32(ROPE_THETA), -jnp.arange(0, DA_ROT, 2, dtype=F32) / DA_ROT)
    ang = positions.astype(F32)[..., None] * inv_freq
    cos, sin = jnp.cos(ang), jnp.sin(ang)
    rest = DA_HD - DA_ROT
    ones = jnp.ones(cos.shape[:-1] + (rest,), F32)
    cos_h = jnp.concatenate([cos, cos, ones], axis=-1)
    sin_h = jnp.concatenate([-sin, sin, 0.0 * ones], axis=-1)
    b, s = positions.shape
    tile = lambda t: jnp.tile(t, (1, 1, LANES // DA_HD)).reshape(b * s, LANES)
    return tile(cos_h), tile(sin_h)


def _relayout_w_in(w_in):
    a_end = PA_W
    wk = HEADS * GLA_DK
    b_qkv = w_in[..., a_end:a_end + 2 * wk + GROUP_WIDTH]
    lr0 = a_end + 2 * wk + GROUP_WIDTH
    b_lr = w_in[..., lr0:lr0 + GLA_RANK]
    b_r = w_in[..., lr0 + GLA_RANK:lr0 + GLA_RANK + GROUP_WIDTH]
    c0 = lr0 + GLA_RANK + GROUP_WIDTH
    rest = w_in[..., c0:]
    pad = jnp.zeros(w_in.shape[:-1] + (LANES - GLA_RANK,), w_in.dtype)
    return jnp.concatenate([w_in[..., :a_end], b_qkv, b_r, b_lr, pad, rest], axis=-1).astype(BF16)


def kernel(x, mem, positions, hgrn_lb_logits, ffn1_norm, ffn1_w_up, ffn1_w_down, mix_norm, w_in, hgrn_out_norm, gla_gate_w, gla_gate_b, gla_out_norm, conv_w, conv_b, conv_ln_g, conv_ln_b, w_out, cross_norm, mem_norm, cross_wq, cross_wkv, cross_wo, ffn2_norm, ffn2_w_up, ffn2_w_down, final_norm):
    batch, seq, d = x.shape
    mem_len = mem.shape[1]
    depth = w_in.shape[0]
    assert d == D_MODEL and seq % (DA_PATTERNS[-1][1] * DA_STEPS) == 0
    assert (batch * seq) % FFN_TILE == 0 and seq % ROW_TILE == 0 and (batch * mem_len) % ROW_TILE == 0

    cos_t, sin_t = _rope_tables(positions)
    p_lb = jax.nn.softmax(hgrn_lb_logits.astype(F32), axis=0)
    lower_bounds = jnp.cumsum(p_lb, axis=0) - p_lb[0:1]

    row = lambda v: v.reshape(1, -1).astype(F32)
    x2 = x.reshape(batch * seq, d)
    mem2 = mem.reshape(batch * mem_len, d)
    up1, down1 = ffn1_w_up.astype(BF16), ffn1_w_down.astype(BF16)
    up2, down2 = ffn2_w_up.astype(BF16), ffn2_w_down.astype(BF16)
    w_in_b, w_out_b = _relayout_w_in(w_in), w_out.astype(BF16)
    wq_b, wkv_b, wo_b = cross_wq.astype(BF16), cross_wkv.astype(BF16), cross_wo.astype(BF16)
    for l in range(depth):
        x2 = _ffn(x2, row(ffn1_norm[l]), up1, down1, l)

        pa, pb, pc, pd = _inproj(x2, row(mix_norm[l]), w_in_b, l)
        gate_w = jnp.zeros((LANES, HEADS * GLA_DK), F32).at[:GLA_RANK].set(gla_gate_w[l])
        o_a = _hgrn(pa, row(lower_bounds[l]), row(hgrn_out_norm[l]), batch, seq)
        o_b = _gla(pb, gate_w, row(gla_gate_b[l]), row(gla_out_norm[l]), batch, seq)
        o_c = _dilated(pc, cos_t, sin_t, batch, seq)
        o_d = _conv(pd, conv_w[l], row(conv_b[l]), row(conv_ln_g[l]), row(conv_ln_b[l]), batch, seq)
        kv = _memkv(mem2, row(mem_norm[l]), wkv_b, l)
        x2 = _cross(x2, (o_a, o_b, o_c, o_d), w_out_b, row(cross_norm[l]), wq_b, kv, wo_b, l, seq, mem_len)

        last = l == depth - 1
        x2 = _ffn(x2, row(ffn2_norm[l]), up2, down2, l, final_g=row(final_norm) if last else None)
    return x2.reshape(batch, seq, d)
```

```python
import functools
import math

import numpy as np
import jax
import jax.numpy as jnp
from jax import lax
from jax.experimental import pallas as pl
from jax.experimental.pallas import tpu as pltpu

F32 = jnp.float32
BF16 = jnp.bfloat16

D_MODEL = 1024
GROUP_WIDTH = D_MODEL // 4
HEADS = 4
HG_DK = GROUP_WIDTH // HEADS
GLA_DK = GROUP_WIDTH // (2 * HEADS)
HEAD_DV = GROUP_WIDTH // HEADS
GLA_RANK = 16
GLA_TAU = 16.0
HG_LB_FLOOR = 1e-20
DA_HD = GROUP_WIDTH // HEADS
DA_ROT = DA_HD // 4
ROPE_THETA = 500000.0
DA_PATTERNS = ((128, 1), (512, 4), (2048, 16))
MASK_VALUE = -1e30
LOG2_E = math.log2(math.e)
CONV_K = 31
X_HD = D_MODEL // HEADS
D_FF = ((int(8 * D_MODEL / 3) + 255) // 256) * 256
EPS = 1e-6

LANES = 128
SUBLANES = 8
ROW_TILE = 512
FFN_TILE = 1024
MXU_DIM = 256
FF_CHUNK = 4 * MXU_DIM
GLA_CHUNK = 64
GLA_UNROLL = 4
DA_STEPS = 128
DA_UNROLL = 4
CONV_TILE = 128
CONV_PAD = 32
VMEM_LIMIT = 56 * 1024 * 1024

PA_W = 4 * GROUP_WIDTH
PB_W = 2 * HEADS * GLA_DK + 2 * GROUP_WIDTH + LANES
PC_W = 3 * GROUP_WIDTH
PD_W = 2 * GROUP_WIDTH


def _dot(a, b):
    return jnp.dot(a, b, preferred_element_type=F32)


def _dot_t(a, b):
    return lax.dot_general(a, b, (((1,), (1,)), ((), ())), preferred_element_type=F32)


def _tdot(a, b):
    return lax.dot_general(a, b, (((0,), (0,)), ((), ())), preferred_element_type=F32)


def _rms(x, g):
    ms = jnp.mean(x * x, axis=-1, keepdims=True)
    return x * lax.rsqrt(ms + EPS) * g


def _split3(x):
    hi = x.astype(BF16)
    r1 = x - hi.astype(F32)
    mid = r1.astype(BF16)
    lo = (r1 - mid.astype(F32)).astype(BF16)
    return hi, mid, lo


def _const_spec(shape):
    nd = len(shape)
    return pl.BlockSpec(shape, lambda *_: (0,) * nd, pipeline_mode=pl.Buffered(1))


def _layer_spec(stacked, layer):
    nd = stacked.ndim - 1
    return pl.BlockSpec((None,) + stacked.shape[1:], lambda *_: (layer,) + (0,) * nd,
                        pipeline_mode=pl.Buffered(1))


def _params(sem):
    return pltpu.CompilerParams(dimension_semantics=sem, vmem_limit_bytes=VMEM_LIMIT)


def _ffn_body(x_ref, g_ref, wup_ref, wd_ref):
    x = x_ref[...]
    h = _rms(x, g_ref[...]).astype(BF16)
    acc = jnp.zeros_like(x)
    for lo in range(0, D_FF, FF_CHUNK):
        width = min(FF_CHUNK, D_FF - lo)
        gate = _dot(h, wup_ref[:, lo:lo + width])
        up = _dot(h, wup_ref[:, D_FF + lo:D_FF + lo + width])
        a = (gate * jax.nn.sigmoid(gate) * up).astype(BF16)
        acc = acc + _dot(a, wd_ref[lo:lo + width, :])
    return x + 0.5 * acc


def _ffn_kernel(x_ref, g_ref, wup_ref, wd_ref, o_ref):
    o_ref[...] = _ffn_body(x_ref, g_ref, wup_ref, wd_ref)


def _ffn_final_kernel(x_ref, g_ref, wup_ref, wd_ref, fg_ref, o_ref):
    o_ref[...] = _rms(_ffn_body(x_ref, g_ref, wup_ref, wd_ref), fg_ref[...])


def _ffn(x2, g, wup, wd, layer, final_g=None):
    n = x2.shape[0]
    row = pl.BlockSpec((FFN_TILE, D_MODEL), lambda i: (i, 0))
    in_specs = [row, _const_spec((1, D_MODEL)), _layer_spec(wup, layer), _layer_spec(wd, layer)]
    args = [x2, g, wup, wd]
    body = _ffn_kernel
    if final_g is not None:
        in_specs.append(_const_spec((1, D_MODEL)))
        args.append(final_g)
        body = _ffn_final_kernel
    return pl.pallas_call(
        body, grid=(n // FFN_TILE,), in_specs=in_specs, out_specs=row,
        out_shape=jax.ShapeDtypeStruct(x2.shape, F32),
        compiler_params=_params(("parallel",)), name="ffn")(*args)


def _inproj_kernel(x_ref, g_ref, w_ref, pa_ref, pb_ref, pc_ref, pd_ref):
    h = _rms(x_ref[...], g_ref[...]).astype(BF16)
    off = 0
    for ref, width in ((pa_ref, PA_W), (pb_ref, PB_W), (pc_ref, PC_W), (pd_ref, PD_W)):
        ref[...] = _dot(h, w_ref[:, off:off + width])
        off += width


def _inproj(x2, g, w, layer):
    n = x2.shape[0]
    widths = (PA_W, PB_W, PC_W, PD_W)
    return pl.pallas_call(
        _inproj_kernel, grid=(n // ROW_TILE,),
        in_specs=[pl.BlockSpec((ROW_TILE, D_MODEL), lambda i: (i, 0)),
                  _const_spec((1, D_MODEL)), _layer_spec(w, layer)],
        out_specs=[pl.BlockSpec((ROW_TILE, wd), lambda i: (i, 0)) for wd in widths],
        out_shape=[jax.ShapeDtypeStruct((n, wd), F32) for wd in widths],
        compiler_params=_params(("parallel",)), name="inproj")(x2, g, w)


def _memkv_kernel(m_ref, g_ref, w_ref, kv_ref):
    h = _rms(m_ref[...], g_ref[...]).astype(BF16)
    kv_ref[...] = _dot(h, w_ref[...]).astype(BF16)


def _memkv(mem2, g, w, layer):
    n = mem2.shape[0]
    return pl.pallas_call(
        _memkv_kernel, grid=(n // ROW_TILE,),
        in_specs=[pl.BlockSpec((ROW_TILE, D_MODEL), lambda i: (i, 0)),
                  _const_spec((1, D_MODEL)), _layer_spec(w, layer)],
        out_specs=pl.BlockSpec((ROW_TILE, 2 * D_MODEL), lambda i: (i, 0)),
        out_shape=jax.ShapeDtypeStruct((n, 2 * D_MODEL), BF16),
        compiler_params=_params(("parallel",)), name="memkv")(mem2, g, w)


def _cross_kernel(x_ref, oa_ref, ob_ref, oc_ref, od_ref, wout_ref, g_ref, wq_ref, kv_ref, wo_ref, o_ref):
    x = x_ref[...]
    for j, ref in enumerate((oa_ref, ob_ref, oc_ref, od_ref)):
        x = x + _dot(ref[...], wout_ref[j * GROUP_WIDTH:(j + 1) * GROUP_WIDTH, :])
    h = _rms(x, g_ref[...]).astype(BF16)
    q = _dot(h, wq_ref[...]).astype(BF16)
    heads = []
    for hd in range(HEADS):
        lo = hd * X_HD
        s = _dot_t(q[:, lo:lo + X_HD], kv_ref[:, lo:lo + X_HD]) * (X_HD ** -0.5)
        s = s - jnp.max(s, axis=-1, keepdims=True)
        p = jnp.exp(s)
        p = p / jnp.sum(p, axis=-1, keepdims=True)
        heads.append(_dot(p.astype(BF16), kv_ref[:, D_MODEL + lo:D_MODEL + lo + X_HD]).astype(BF16))
    o = jnp.concatenate(heads, axis=-1)
    o_ref[...] = x + _dot(o, wo_ref[...])


def _cross(x2, outs, w_out, g, wq, kv, wo, layer, seq, mem_len):
    n = x2.shape[0]
    tiles_per_seq = seq // ROW_TILE
    row = pl.BlockSpec((ROW_TILE, D_MODEL), lambda i: (i, 0))
    grp = pl.BlockSpec((ROW_TILE, GROUP_WIDTH), lambda i: (i, 0))
    return pl.pallas_call(
        _cross_kernel, grid=(n // ROW_TILE,),
        in_specs=[row, grp, grp, grp, grp, _layer_spec(w_out, layer),
                  _const_spec((1, D_MODEL)), _layer_spec(wq, layer),
                  pl.BlockSpec((mem_len, 2 * D_MODEL), lambda i: (i // tiles_per_seq, 0)),
                  _layer_spec(wo, layer)],
        out_specs=row, out_shape=jax.ShapeDtypeStruct(x2.shape, F32),
        compiler_params=_params(("parallel",)), name="cross")(x2, *outs, w_out, g, wq, kv, wo)


def _gla_constants(dk):
    c, h = GLA_CHUNK, HEADS
    w, wv = h * dk, h * HEAD_DV
    r = np.arange(c)
    blocks = [(r[:, None] >= r[None, :])]
    masks = [(r[:, None] == r[None, :])]
    m = 1
    while m < c:
        ref = (r // (2 * m)) * (2 * m) + m - 1
        upper = (r % (2 * m)) >= m
        rp = r[None, :]
        rng_up = (rp > ref[:, None]) & (rp <= r[:, None])
        rng_lo = (rp > r[:, None]) & (rp <= ref[:, None])
        blocks.append(np.where(upper[:, None], rng_up, rng_lo))
        same = (r[:, None] // (2 * m)) == (r[None, :] // (2 * m))
        masks.append(same & upper[:, None] & (~upper)[None, :])
        m *= 2
    nmat = np.concatenate(blocks, axis=0).astype(np.float32)
    lvl = np.stack([np.tile(mk, (1, h)) for mk in masks]).astype(np.float32)
    rows = np.arange(h * c)[:, None] // c
    hm_k = (rows == (np.arange(w)[None, :] // dk)).astype(np.float32)
    hm_v = (rows == (np.arange(wv)[None, :] // HEAD_DV)).astype(np.float32)
    bm_t = ((np.arange(wv)[:, None] // HEAD_DV) == (np.arange(w)[None, :] // dk)).astype(np.float32)
    ones_blk = ((np.arange(wv)[:, None] // HEAD_DV) == (np.arange(wv)[None, :] // HEAD_DV))
    return (jnp.asarray(nmat, BF16), jnp.asarray(lvl, F32), jnp.asarray(hm_k, BF16),
            jnp.asarray(hm_v, BF16), jnp.asarray(bm_t, F32), jnp.asarray(ones_blk.astype(np.float32), BF16))


def _gla_recurrence(seq, q_ref, k_ref, v_ref, g_ref, o_ref, st_ref,
                    nmat_ref, lvl_ref, hmk_ref, hmv_ref, bmt_ref):
    c, h = GLA_CHUNK, HEADS
    n_lvl = lvl_ref.shape[0]
    st_ref[...] = jnp.zeros_like(st_ref)

    def step(ci, carry):
        r0 = pl.multiple_of(ci * c, c)
        q = q_ref[pl.ds(r0, c), :]
        k = k_ref[pl.ds(r0, c), :]
        v = v_ref[pl.ds(r0, c), :]
        g = g_ref[pl.ds(r0, c), :]
        g_hi = g.astype(BF16)
        g_mid = (g - g_hi.astype(F32)).astype(BF16)
        cum = nmat_ref[0:c, :]
        b = _dot(cum, g_hi) + _dot(cum, g_mid)
        lvl_rows = nmat_ref[c:, :]
        sums = _dot(lvl_rows, g_hi)
        hmk = hmk_ref[...]
        acc = jnp.zeros((c, h * c), F32)
        for li in range(n_lvl):
            if li == 0:
                qt, kt = q, k
            else:
                e = jnp.exp(sums[(li - 1) * c:li * c])
                qt, kt = q * e, k * e
            kst = jnp.concatenate([kt.astype(BF16)] * h, axis=0) * hmk
            acc = acc + _dot_t(qt.astype(BF16), kst) * lvl_ref[li]
        vb = v.astype(BF16)
        vst = jnp.concatenate([vb] * h, axis=0) * hmv_ref[...]
        o = _dot(acc.astype(BF16), vst)
        st = st_ref[...]
        o = o + _dot_t((q * jnp.exp(b)).astype(BF16), st.astype(BF16))
        b_last = b[c - 1:c, :]
        kh = (k * jnp.exp(b_last - b)).astype(BF16)
        st_ref[...] = st * jnp.exp(b_last) + _tdot(vb, kh) * bmt_ref[...]
        o_ref[pl.ds(r0, c), :] = o
        return carry

    lax.fori_loop(0, seq // c, step, 0, unroll=GLA_UNROLL)


def _head_rms_gate(o, gain, gate, ones_blk):
    sq = o * o
    hi = sq.astype(BF16)
    lo = (sq - hi.astype(F32)).astype(BF16)
    ms = (_dot(hi, ones_blk) + _dot(lo, ones_blk)) * (1.0 / HEAD_DV)
    return o * lax.rsqrt(ms + EPS) * gain * (gate * jax.nn.sigmoid(gate))


def _hgrn_kernel(p_ref, lb_ref, gain_ref, nmat_ref, lvl_ref, hmk_ref, hmv_ref, bmt_ref, ones_ref,
                 o_ref, q_s, k_s, g_s, o_s, st_s):
    seq, gw = o_ref.shape
    z = p_ref[:, gw:2 * gw]
    lb = lb_ref[...]
    lbf = jnp.maximum(lb, HG_LB_FLOOR)
    e = jnp.exp(-jnp.abs(z))
    inv = 1.0 / (1.0 + e)
    sig = jnp.where(z >= 0, inv, e * inv)
    nsig = jnp.where(z >= 0, e * inv, inv)
    f = lbf + (1.0 - lb) * sig
    g_s[...] = jnp.log(f)
    k_s[...] = (1.0 - lb) * nsig - (lbf - lb)
    q_s[...] = p_ref[:, 0:gw]
    _gla_recurrence(seq, q_s, k_s, p_ref.at[:, 2 * gw:3 * gw], g_s, o_s, st_s,
                    nmat_ref, lvl_ref, hmk_ref, hmv_ref, bmt_ref)
    o_ref[...] = _head_rms_gate(o_s[...], gain_ref[...], p_ref[:, 3 * gw:4 * gw],
                                ones_ref[...]).astype(o_ref.dtype)


def _gla_kernel(p_ref, gw_ref, gb_ref, gain_ref, nmat_ref, lvl_ref, hmk_ref, hmv_ref, bmt_ref,
                ones_ref, o_ref, q_s, g_s, o_s, st_s):
    seq, gw = o_ref.shape
    wk = HEADS * GLA_DK
    lr = p_ref[:, 2 * wk + 2 * gw:2 * wk + 2 * gw + LANES]
    lr_hi = lr.astype(BF16)
    lr_lo = (lr - lr_hi.astype(F32)).astype(BF16)
    w = gw_ref[...]
    w_hi = w.astype(BF16)
    w_lo = (w - w_hi.astype(F32)).astype(BF16)
    y = _dot(lr_hi, w_hi) + _dot(lr_hi, w_lo) + _dot(lr_lo, w_hi) + gb_ref[...]
    g_s[...] = (jnp.minimum(y, 0.0) - jnp.log(1.0 + jnp.exp(-jnp.abs(y)))) * (1.0 / GLA_TAU)
    q_s[...] = p_ref[:, 0:wk] * (GLA_DK ** -0.5)
    _gla_recurrence(seq, q_s, p_ref.at[:, wk:2 * wk], p_ref.at[:, 2 * wk:2 * wk + gw], g_s, o_s, st_s,
                    nmat_ref, lvl_ref, hmk_ref, hmv_ref, bmt_ref)
    o_ref[...] = _head_rms_gate(o_s[...], gain_ref[...], p_ref[:, 2 * wk + gw:2 * wk + 2 * gw],
                                ones_ref[...]).astype(o_ref.dtype)


def _seq_spec(seq, width):
    return pl.BlockSpec((seq, width), lambda b: (b, 0))


def _hgrn(pa, lb, gain, batch, seq):
    consts = _gla_constants(HG_DK)
    w = HEADS * HG_DK
    return pl.pallas_call(
        _hgrn_kernel, grid=(batch,),
        in_specs=[_seq_spec(seq, PA_W), _const_spec((1, w)), _const_spec((1, GROUP_WIDTH))]
                 + [_const_spec(cst.shape) for cst in consts],
        out_specs=_seq_spec(seq, GROUP_WIDTH),
        out_shape=jax.ShapeDtypeStruct((batch * seq, GROUP_WIDTH), BF16),
        scratch_shapes=[pltpu.VMEM((seq, w), F32), pltpu.VMEM((seq, w), F32), pltpu.VMEM((seq, w), F32),
                        pltpu.VMEM((seq, GROUP_WIDTH), F32), pltpu.VMEM((GROUP_WIDTH, w), F32)],
        compiler_params=_params(("parallel",)), name="hgrn")(pa, lb, gain, *consts)


def _gla(pb, gate_w, gate_b, gain, batch, seq):
    consts = _gla_constants(GLA_DK)
    w = HEADS * GLA_DK
    return pl.pallas_call(
        _gla_kernel, grid=(batch,),
        in_specs=[_seq_spec(seq, PB_W), _const_spec((LANES, w)), _const_spec((1, w)),
                  _const_spec((1, GROUP_WIDTH))] + [_const_spec(cst.shape) for cst in consts],
        out_specs=_seq_spec(seq, GROUP_WIDTH),
        out_shape=jax.ShapeDtypeStruct((batch * seq, GROUP_WIDTH), BF16),
        scratch_shapes=[pltpu.VMEM((seq, w), F32), pltpu.VMEM((seq, w), F32),
                        pltpu.VMEM((seq, GROUP_WIDTH), F32), pltpu.VMEM((GROUP_WIDTH, w), F32)],
        compiler_params=_params(("parallel",)), name="gla")(pb, gate_w, gate_b, gain, *consts)


def _dilated_kernel(p_ref, cos_ref, sin_ref, o_ref, q_s, k_s, v_s, op_s, lse_s):
    seq, gw = o_ref.shape
    n_half = gw // LANES
    heads_per_half = LANES // DA_HD
    lane = lax.broadcasted_iota(jnp.int32, (1, LANES), 1)
    in_head = lane & (DA_HD - 1)
    first_half = in_head < (DA_ROT // 2)
    cos, sin = cos_ref[...], sin_ref[...]

    def rope(t):
        partner = jnp.where(first_half, pltpu.roll(t, LANES - DA_ROT // 2, 1),
                            pltpu.roll(t, DA_ROT // 2, 1))
        return t * cos + partner * sin

    for hf in range(n_half):
        lo = hf * LANES
        q_s[hf] = rope(p_ref[:, lo:lo + LANES]) * (DA_HD ** -0.5 * LOG2_E)
        k_s[hf] = rope(p_ref[:, gw + lo:gw + lo + LANES])
        v_s[hf] = p_ref[:, 2 * gw + lo:2 * gw + lo + LANES]

    nq = DA_STEPS
    qi = lax.broadcasted_iota(jnp.int32, (nq, 2 * nq), 0)
    kj = lax.broadcasted_iota(jnp.int32, (nq, 2 * nq), 1)
    cur_ok = (kj >= nq) & (kj - nq <= qi)
    head_of_lane = lane >> int(math.log2(DA_HD))

    for pi, (window, dil) in enumerate(DA_PATTERNS):
        n_blk = seq // (dil * nq)

        def block(i, carry, pi=pi, dil=dil, n_blk=n_blk):
            r = i // n_blk
            n = i % n_blk
            cur0 = r + n * (nq * dil)
            prev0 = r + jnp.maximum(n - 1, 0) * (nq * dil)

            def rows(start):
                if dil == 1:
                    return pl.ds(pl.multiple_of(start, nq), nq)
                return pl.ds(start, nq, stride=dil)

            first_row = qi + jnp.where(n > 0, 0, nq)
            valid = cur_ok | ((kj < nq) & (kj >= first_row))
            for hf in range(n_half):
                q = q_s[hf, rows(cur0), :]
                kwin = jnp.concatenate([k_s[hf, rows(prev0), :], k_s[hf, rows(cur0), :]], axis=0).astype(BF16)
                vwin = jnp.concatenate([v_s[hf, rows(prev0), :], v_s[hf, rows(cur0), :]], axis=0).astype(BF16)
                o_acc = jnp.zeros((nq, LANES), F32)
                lse_acc = jnp.zeros((nq, LANES), F32)
                for hd in range(heads_per_half):
                    hm = head_of_lane == hd
                    s = _dot_t(jnp.where(hm, q, 0.0).astype(BF16), kwin)
                    s = jnp.where(valid, s, MASK_VALUE)
                    m = jnp.max(s, axis=-1, keepdims=True)
                    p = jnp.exp2(s - m)
                    l = jnp.sum(p, axis=-1, keepdims=True)
                    oh = _dot(p.astype(BF16), vwin) / l
                    o_acc = jnp.where(hm, oh, o_acc)
                    lse_acc = jnp.where(hm, m * (1.0 / LOG2_E) + jnp.log(l), lse_acc)
                op_s[pi, hf, rows(cur0), :] = o_acc
                lse_s[pi, hf, rows(cur0), :] = lse_acc
            return carry

        lax.fori_loop(0, dil * n_blk, block, 0, unroll=DA_UNROLL)

    for hf in range(n_half):
        lses = [lse_s[pi, hf] for pi in range(len(DA_PATTERNS))]
        mx = functools.reduce(jnp.maximum, lses)
        ws = [jnp.exp(l - mx) for l in lses]
        num = sum(wt * op_s[pi, hf] for pi, wt in enumerate(ws))
        o_ref[:, hf * LANES:(hf + 1) * LANES] = (num / sum(ws)).astype(o_ref.dtype)


def _dilated(pc, cos_t, sin_t, batch, seq):
    gw = GROUP_WIDTH
    n_pat = len(DA_PATTERNS)
    n_half = gw // LANES
    return pl.pallas_call(
        _dilated_kernel, grid=(batch,),
        in_specs=[_seq_spec(seq, PC_W), _seq_spec(seq, LANES), _seq_spec(seq, LANES)],
        out_specs=_seq_spec(seq, gw),
        out_shape=jax.ShapeDtypeStruct((batch * seq, gw), BF16),
        scratch_shapes=[pltpu.VMEM((n_half, seq, LANES), F32)] * 3
                       + [pltpu.VMEM((n_pat, n_half, seq, LANES), F32)] * 2,
        compiler_params=_params(("parallel",)), name="dilated")(pc, cos_t, sin_t)


def _conv_kernel(p_ref, w_ref, b_ref, lg_ref, lb_ref, o_ref, u_s):
    seq, gw = o_ref.shape
    a = p_ref[:, 0:gw]
    gate = p_ref[:, gw:2 * gw]
    u_s[0:CONV_PAD, :] = jnp.zeros((CONV_PAD, gw), F32)
    u_s[CONV_PAD:CONV_PAD + seq, :] = a * jax.nn.sigmoid(gate)
    u_s[CONV_PAD + seq:CONV_PAD + seq + SUBLANES, :] = jnp.zeros((SUBLANES, gw), F32)
    w = w_ref[...]
    first = CONV_PAD - (CONV_K - 1)
    ext = CONV_TILE + SUBLANES

    def tile(ti, carry):
        t0 = pl.multiple_of(ti * CONV_TILE, CONV_TILE)
        y = jnp.zeros((CONV_TILE, gw), F32) + b_ref[...]
        for s in range(SUBLANES):
            z = None
            for a in range(s, first + CONV_K, SUBLANES):
                if a >= first:
                    term = u_s[pl.ds(pl.multiple_of(t0 + (a - s), SUBLANES), ext), :] * w[a - first:a - first + 1, :]
                    z = term if z is None else z + term
            y = y + z[s:s + CONV_TILE, :]
        mu = jnp.mean(y, axis=-1, keepdims=True)
        d = y - mu
        var = jnp.mean(d * d, axis=-1, keepdims=True)
        yn = d * lax.rsqrt(var + EPS) * lg_ref[...] + lb_ref[...]
        o_ref[pl.ds(t0, CONV_TILE), :] = (yn * jax.nn.sigmoid(yn)).astype(o_ref.dtype)
        return carry

    lax.fori_loop(0, seq // CONV_TILE, tile, 0)


def _conv(pd, w, b, ln_g, ln_b, batch, seq):
    gw = GROUP_WIDTH
    vec = _const_spec((1, gw))
    return pl.pallas_call(
        _conv_kernel, grid=(batch,),
        in_specs=[_seq_spec(seq, PD_W), _const_spec(w.shape), vec, vec, vec],
        out_specs=_seq_spec(seq, gw),
        out_shape=jax.ShapeDtypeStruct((batch * seq, gw), BF16),
        scratch_shapes=[pltpu.VMEM((CONV_PAD + seq + SUBLANES, gw), F32)],
        compiler_params=_params(("parallel",)), name="conv")(pd, w, b, ln_g, ln_b)


def _rope_tables(positions):
    half = DA_ROT // 2
    inv_freq = jnp.power(jnp.float32(ROPE_THETA), -jnp.arange(0, DA_ROT, 2, dtype=F32) / DA_ROT)
    ang = positions.astype(F32)[..., None] * inv_freq
    cos, sin = jnp.cos(ang), jnp.sin(ang)
    rest = DA_HD - DA_ROT
    ones = jnp.ones(cos.shape[:-1] + (rest,), F32)
    cos_h = jnp.concatenate([cos, cos, ones], axis=-1)
    sin_h = jnp.concatenate([-sin, sin, 0.0 * ones], axis=-1)
    b, s = positions.shape
    tile = lambda t: jnp.tile(t, (1, 1, LANES // DA_HD)).reshape(b * s, LANES)
    return tile(cos_h), tile(sin_h)


def _relayout_w_in(w_in):
    a_end = PA_W
    wk = HEADS * GLA_DK
    b_qkv = w_in[..., a_end:a_end + 2 * wk + GROUP_WIDTH]
    lr0 = a_end + 2 * wk + GROUP_WIDTH
    b_lr = w_in[..., lr0:lr0 + GLA_RANK]
    b_r = w_in[..., lr0 + GLA_RANK:lr0 + GLA_RANK + GROUP_WIDTH]
    c0 = lr0 + GLA_RANK + GROUP_WIDTH
    rest = w_in[..., c0:]
    pad = jnp.zeros(w_in.shape[:-1] + (LANES - GLA_RANK,), w_in.dtype)
    return jnp.concatenate([w_in[..., :a_end], b_qkv, b_r, b_lr, pad, rest], axis=-1).astype(BF16)


def kernel(x, mem, positions, hgrn_lb_logits, ffn1_norm, ffn1_w_up, ffn1_w_down, mix_norm, w_in, hgrn_out_norm, gla_gate_w, gla_gate_b, gla_out_norm, conv_w, conv_b, conv_ln_g, conv_ln_b, w_out, cross_norm, mem_norm, cross_wq, cross_wkv, cross_wo, ffn2_norm, ffn2_w_up, ffn2_w_down, final_norm):
    batch, seq, d = x.shape
    mem_len = mem.shape[1]
    depth = w_in.shape[0]
    assert d == D_MODEL and seq % (DA_PATTERNS[-1][1] * DA_STEPS) == 0
    assert (batch * seq) % FFN_TILE == 0 and seq % ROW_TILE == 0 and (batch * mem_len) % ROW_TILE == 0

    cos_t, sin_t = _rope_tables(positions)
    p_lb = jax.nn.softmax(hgrn_lb_logits.astype(F32), axis=0)
    lower_bounds = jnp.cumsum(p_lb, axis=0) - p_lb[0:1]

    row = lambda v: v.reshape(1, -1).astype(F32)
    x2 = x.reshape(batch * seq, d)
    mem2 = mem.reshape(batch * mem_len, d)
    up1, down1 = ffn1_w_up.astype(BF16), ffn1_w_down.astype(BF16)
    up2, down2 = ffn2_w_up.astype(BF16), ffn2_w_down.astype(BF16)
    w_in_b, w_out_b = _relayout_w_in(w_in), w_out.astype(BF16)
    wq_b, wkv_b, wo_b = cross_wq.astype(BF16), cross_wkv.astype(BF16), cross_wo.astype(BF16)
    for l in range(depth):
        x2 = _ffn(x2, row(ffn1_norm[l]), up1, down1, l)

        pa, pb, pc, pd = _inproj(x2, row(mix_norm[l]), w_in_b, l)
        gate_w = jnp.zeros((LANES, HEADS * GLA_DK), F32).at[:GLA_RANK].set(gla_gate_w[l])
        o_a = _hgrn(pa, row(lower_bounds[l]), row(hgrn_out_norm[l]), batch, seq)
        o_b = _gla(pb, gate_w, row(gla_gate_b[l]), row(gla_out_norm[l]), batch, seq)
        o_c = _dilated(pc, cos_t, sin_t, batch, seq)
        o_d = _conv(pd, conv_w[l], row(conv_b[l]), row(conv_ln_g[l]), row(conv_ln_b[l]), batch, seq)
        kv = _memkv(mem2, row(mem_norm[l]), wkv_b, l)
        x2 = _cross(x2, (o_a, o_b, o_c, o_d), w_out_b, row(cross_norm[l]), wq_b, kv, wo_b, l, seq, mem_len)

        last = l == depth - 1
        x2 = _ffn(x2, row(ffn2_norm[l]), up2, down2, l, final_g=row(final_norm) if last else None)
    return x2.reshape(batch, seq, d)
```

```python
import functools
import math

import numpy as np
import jax
import jax.numpy as jnp
from jax import lax
from jax.experimental import pallas as pl
from jax.experimental.pallas import tpu as pltpu

F32 = jnp.float32
BF16 = jnp.bfloat16

D_MODEL = 1024
GROUP_WIDTH = D_MODEL // 4
HEADS = 4
HG_DK = GROUP_WIDTH // HEADS
GLA_DK = GROUP_WIDTH // (2 * HEADS)
HEAD_DV = GROUP_WIDTH // HEADS
GLA_RANK = 16
GLA_TAU = 16.0
HG_LB_FLOOR = 1e-20
DA_HD = GROUP_WIDTH // HEADS
DA_ROT = DA_HD // 4
ROPE_THETA = 500000.0
DA_PATTERNS = ((128, 1), (512, 4), (2048, 16))
MASK_VALUE = -1e30
LOG2_E = math.log2(math.e)
CONV_K = 31
X_HD = D_MODEL // HEADS
D_FF = ((int(8 * D_MODEL / 3) + 255) // 256) * 256
EPS = 1e-6

LANES = 128
SUBLANES = 8
ROW_TILE = 512
FFN_TILE = 1024
MXU_DIM = 256
FF_CHUNK = 4 * MXU_DIM
GLA_CHUNK = 64
GLA_UNROLL = 8
DA_STEPS = 128
DA_UNROLL = 8
CONV_TILE = 128
CONV_PAD = 32
VMEM_LIMIT = 56 * 1024 * 1024

PA_W = 4 * GROUP_WIDTH
PB_W = 2 * HEADS * GLA_DK + 2 * GROUP_WIDTH + LANES
PC_W = 3 * GROUP_WIDTH
PD_W = 2 * GROUP_WIDTH


def _dot(a, b):
    return jnp.dot(a, b, preferred_element_type=F32)


def _dot_t(a, b):
    return lax.dot_general(a, b, (((1,), (1,)), ((), ())), preferred_element_type=F32)


def _tdot(a, b):
    return lax.dot_general(a, b, (((0,), (0,)), ((), ())), preferred_element_type=F32)


def _rms(x, g):
    ms = jnp.mean(x * x, axis=-1, keepdims=True)
    return x * lax.rsqrt(ms + EPS) * g


def _split3(x):
    hi = x.astype(BF16)
    r1 = x - hi.astype(F32)
    mid = r1.astype(BF16)
    lo = (r1 - mid.astype(F32)).astype(BF16)
    return hi, mid, lo


def _const_spec(shape):
    nd = len(shape)
    return pl.BlockSpec(shape, lambda *_: (0,) * nd, pipeline_mode=pl.Buffered(1))


def _layer_spec(stacked, layer):
    nd = stacked.ndim - 1
    return pl.BlockSpec((None,) + stacked.shape[1:], lambda *_: (layer,) + (0,) * nd,
                        pipeline_mode=pl.Buffered(1))


def _params(sem):
    return pltpu.CompilerParams(dimension_semantics=sem, vmem_limit_bytes=VMEM_LIMIT)


def _ffn_body(x_ref, g_ref, wup_ref, wd_ref):
    x = x_ref[...]
    h = _rms(x, g_ref[...]).astype(BF16)
    acc = jnp.zeros_like(x)
    for lo in range(0, D_FF, FF_CHUNK):
        width = min(FF_CHUNK, D_FF - lo)
        gate = _dot(h, wup_ref[:, lo:lo + width])
        up = _dot(h, wup_ref[:, D_FF + lo:D_FF + lo + width])
        a = (gate * jax.nn.sigmoid(gate) * up).astype(BF16)
        acc = acc + _dot(a, wd_ref[lo:lo + width, :])
    return x + 0.5 * acc


def _ffn_kernel(x_ref, g_ref, wup_ref, wd_ref, o_ref):
    o_ref[...] = _ffn_body(x_ref, g_ref, wup_ref, wd_ref)


def _ffn_final_kernel(x_ref, g_ref, wup_ref, wd_ref, fg_ref, o_ref):
    o_ref[...] = _rms(_ffn_body(x_ref, g_ref, wup_ref, wd_ref), fg_ref[...])


def _ffn(x2, g, wup, wd, layer, final_g=None):
    n = x2.shape[0]
    row = pl.BlockSpec((FFN_TILE, D_MODEL), lambda i: (i, 0))
    in_specs = [row, _const_spec((1, D_MODEL)), _layer_spec(wup, layer), _layer_spec(wd, layer)]
    args = [x2, g, wup, wd]
    body = _ffn_kernel
    if final_g is not None:
        in_specs.append(_const_spec((1, D_MODEL)))
        args.append(final_g)
        body = _ffn_final_kernel
    return pl.pallas_call(
        body, grid=(n // FFN_TILE,), in_specs=in_specs, out_specs=row,
        out_shape=jax.ShapeDtypeStruct(x2.shape, F32),
        compiler_params=_params(("parallel",)), name="ffn")(*args)


def _inproj_kernel(x_ref, g_ref, w_ref, pa_ref, pb_ref, pc_ref, pd_ref):
    h = _rms(x_ref[...], g_ref[...]).astype(BF16)
    off = 0
    for ref, width in ((pa_ref, PA_W), (pb_ref, PB_W), (pc_ref, PC_W), (pd_ref, PD_W)):
        ref[...] = _dot(h, w_ref[:, off:off + width])
        off += width


def _inproj(x2, g, w, layer):
    n = x2.shape[0]
    widths = (PA_W, PB_W, PC_W, PD_W)
    return pl.pallas_call(
        _inproj_kernel, grid=(n // ROW_TILE,),
        in_specs=[pl.BlockSpec((ROW_TILE, D_MODEL), lambda i: (i, 0)),
                  _const_spec((1, D_MODEL)), _layer_spec(w, layer)],
        out_specs=[pl.BlockSpec((ROW_TILE, wd), lambda i: (i, 0)) for wd in widths],
        out_shape=[jax.ShapeDtypeStruct((n, wd), F32) for wd in widths],
        compiler_params=_params(("parallel",)), name="inproj")(x2, g, w)


def _memkv_kernel(m_ref, g_ref, w_ref, kv_ref):
    h = _rms(m_ref[...], g_ref[...]).astype(BF16)
    kv_ref[...] = _dot(h, w_ref[...]).astype(BF16)


def _memkv(mem2, g, w, layer):
    n = mem2.shape[0]
    return pl.pallas_call(
        _memkv_kernel, grid=(n // ROW_TILE,),
        in_specs=[pl.BlockSpec((ROW_TILE, D_MODEL), lambda i: (i, 0)),
                  _const_spec((1, D_MODEL)), _layer_spec(w, layer)],
        out_specs=pl.BlockSpec((ROW_TILE, 2 * D_MODEL), lambda i: (i, 0)),
        out_shape=jax.ShapeDtypeStruct((n, 2 * D_MODEL), BF16),
        compiler_params=_params(("parallel",)), name="memkv")(mem2, g, w)


def _cross_kernel(x_ref, oa_ref, ob_ref, oc_ref, od_ref, wout_ref, g_ref, wq_ref, kv_ref, wo_ref, o_ref):
    x = x_ref[...]
    for j, ref in enumerate((oa_ref, ob_ref, oc_ref, od_ref)):
        x = x + _dot(ref[...], wout_ref[j * GROUP_WIDTH:(j + 1) * GROUP_WIDTH, :])
    h = _rms(x, g_ref[...]).astype(BF16)
    q = _dot(h, wq_ref[...]).astype(BF16)
    heads = []
    for hd in range(HEADS):
        lo = hd * X_HD
        s = _dot_t(q[:, lo:lo + X_HD], kv_ref[:, lo:lo + X_HD]) * (X_HD ** -0.5)
        s = s - jnp.max(s, axis=-1, keepdims=True)
        p = jnp.exp(s)
        p = p / jnp.sum(p, axis=-1, keepdims=True)
        heads.append(_dot(p.astype(BF16), kv_ref[:, D_MODEL + lo:D_MODEL + lo + X_HD]).astype(BF16))
    o = jnp.concatenate(heads, axis=-1)
    o_ref[...] = x + _dot(o, wo_ref[...])


def _cross(x2, outs, w_out, g, wq, kv, wo, layer, seq, mem_len):
    n = x2.shape[0]
    tiles_per_seq = seq // ROW_TILE
    row = pl.BlockSpec((ROW_TILE, D_MODEL), lambda i: (i, 0))
    grp = pl.BlockSpec((ROW_TILE, GROUP_WIDTH), lambda i: (i, 0))
    return pl.pallas_call(
        _cross_kernel, grid=(n // ROW_TILE,),
        in_specs=[row, grp, grp, grp, grp, _layer_spec(w_out, layer),
                  _const_spec((1, D_MODEL)), _layer_spec(wq, layer),
                  pl.BlockSpec((mem_len, 2 * D_MODEL), lambda i: (i // tiles_per_seq, 0)),
                  _layer_spec(wo, layer)],
        out_specs=row, out_shape=jax.ShapeDtypeStruct(x2.shape, F32),
        compiler_params=_params(("parallel",)), name="cross")(x2, *outs, w_out, g, wq, kv, wo)


def _gla_constants(dk):
    c, h = GLA_CHUNK, HEADS
    w, wv = h * dk, h * HEAD_DV
    r = np.arange(c)
    blocks = [(r[:, None] >= r[None, :])]
    masks = [(r[:, None] == r[None, :])]
    m = 1
    while m < c:
        ref = (r // (2 * m)) * (2 * m) + m - 1
        upper = (r % (2 * m)) >= m
        rp = r[None, :]
        rng_up = (rp > ref[:, None]) & (rp <= r[:, None])
        rng_lo = (rp > r[:, None]) & (rp <= ref[:, None])
        blocks.append(np.where(upper[:, None], rng_up, rng_lo))
        same = (r[:, None] // (2 * m)) == (r[None, :] // (2 * m))
        masks.append(same & upper[:, None] & (~upper)[None, :])
        m *= 2
    nmat = np.concatenate(blocks, axis=0).astype(np.float32)
    lvl = np.stack([np.tile(mk, (1, h)) for mk in masks]).astype(np.float32)
    rows = np.arange(h * c)[:, None] // c
    hm_k = (rows == (np.arange(w)[None, :] // dk)).astype(np.float32)
    hm_v = (rows == (np.arange(wv)[None, :] // HEAD_DV)).astype(np.float32)
    bm_t = ((np.arange(wv)[:, None] // HEAD_DV) == (np.arange(w)[None, :] // dk)).astype(np.float32)
    ones_blk = ((np.arange(wv)[:, None] // HEAD_DV) == (np.arange(wv)[None, :] // HEAD_DV))
    return (jnp.asarray(nmat, BF16), jnp.asarray(lvl, F32), jnp.asarray(hm_k, BF16),
            jnp.asarray(hm_v, BF16), jnp.asarray(bm_t, F32), jnp.asarray(ones_blk.astype(np.float32), BF16))


def _gla_recurrence(seq, q_ref, k_ref, v_ref, g_ref, o_ref, st_ref,
                    nmat_ref, lvl_ref, hmk_ref, hmv_ref, bmt_ref):
    c, h = GLA_CHUNK, HEADS
    n_lvl = lvl_ref.shape[0]
    st_ref[...] = jnp.zeros_like(st_ref)

    def step(ci, carry):
        r0 = pl.multiple_of(ci * c, c)
        q = q_ref[pl.ds(r0, c), :]
        k = k_ref[pl.ds(r0, c), :]
        v = v_ref[pl.ds(r0, c), :]
        g = g_ref[pl.ds(r0, c), :]
        g_hi = g.astype(BF16)
        g_mid = (g - g_hi.astype(F32)).astype(BF16)
        cum = nmat_ref[0:c, :]
        b = _dot(cum, g_hi) + _dot(cum, g_mid)
        lvl_rows = nmat_ref[c:, :]
        sums = _dot(lvl_rows, g_hi)
        hmk = hmk_ref[...]
        acc = jnp.zeros((c, h * c), F32)
        for li in range(n_lvl):
            if li == 0:
                qt, kt = q, k
            else:
                e = jnp.exp(sums[(li - 1) * c:li * c])
                qt, kt = q * e, k * e
            kst = jnp.concatenate([kt.astype(BF16)] * h, axis=0) * hmk
            acc = acc + _dot_t(qt.astype(BF16), kst) * lvl_ref[li]
        vb = v.astype(BF16)
        vst = jnp.concatenate([vb] * h, axis=0) * hmv_ref[...]
        o = _dot(acc.astype(BF16), vst)
        st = st_ref[...]
        o = o + _dot_t((q * jnp.exp(b)).astype(BF16), st.astype(BF16))
        b_last = b[c - 1:c, :]
        kh = (k * jnp.exp(b_last - b)).astype(BF16)
        st_ref[...] = st * jnp.exp(b_last) + _tdot(vb, kh) * bmt_ref[...]
        o_ref[pl.ds(r0, c), :] = o
        return carry

    lax.fori_loop(0, seq // c, step, 0, unroll=GLA_UNROLL)


def _head_rms_gate(o, gain, gate, ones_blk):
    sq = o * o
    hi = sq.astype(BF16)
    lo = (sq - hi.astype(F32)).astype(BF16)
    ms = (_dot(hi, ones_blk) + _dot(lo, ones_blk)) * (1.0 / HEAD_DV)
    return o * lax.rsqrt(ms + EPS) * gain * (gate * jax.nn.sigmoid(gate))


def _hgrn_kernel(p_ref, lb_ref, gain_ref, nmat_ref, lvl_ref, hmk_ref, hmv_ref, bmt_ref, ones_ref,
                 o_ref, q_s, k_s, g_s, o_s, st_s):
    seq, gw = o_ref.shape
    z = p_ref[:, gw:2 * gw]
    lb = lb_ref[...]
    lbf = jnp.maximum(lb, HG_LB_FLOOR)
    e = jnp.exp(-jnp.abs(z))
    inv = 1.0 / (1.0 + e)
    sig = jnp.where(z >= 0, inv, e * inv)
    nsig = jnp.where(z >= 0, e * inv, inv)
    f = lbf + (1.0 - lb) * sig
    g_s[...] = jnp.log(f)
    k_s[...] = (1.0 - lb) * nsig - (lbf - lb)
    q_s[...] = p_ref[:, 0:gw]
    _gla_recurrence(seq, q_s, k_s, p_ref.at[:, 2 * gw:3 * gw], g_s, o_s, st_s,
                    nmat_ref, lvl_ref, hmk_ref, hmv_ref, bmt_ref)
    o_ref[...] = _head_rms_gate(o_s[...], gain_ref[...], p_ref[:, 3 * gw:4 * gw],
                                ones_ref[...]).astype(o_ref.dtype)


def _gla_kernel(p_ref, gw_ref, gb_ref, gain_ref, nmat_ref, lvl_ref, hmk_ref, hmv_ref, bmt_ref,
                ones_ref, o_ref, q_s, g_s, o_s, st_s):
    seq, gw = o_ref.shape
    wk = HEADS * GLA_DK
    lr = p_ref[:, 2 * wk + 2 * gw:2 * wk + 2 * gw + LANES]
    lr_hi = lr.astype(BF16)
    lr_lo = (lr - lr_hi.astype(F32)).astype(BF16)
    w = gw_ref[...]
    w_hi = w.astype(BF16)
    w_lo = (w - w_hi.astype(F32)).astype(BF16)
    y = _dot(lr_hi, w_hi) + _dot(lr_hi, w_lo) + _dot(lr_lo, w_hi) + gb_ref[...]
    g_s[...] = (jnp.minimum(y, 0.0) - jnp.log(1.0 + jnp.exp(-jnp.abs(y)))) * (1.0 / GLA_TAU)
    q_s[...] = p_ref[:, 0:wk] * (GLA_DK ** -0.5)
    _gla_recurrence(seq, q_s, p_ref.at[:, wk:2 * wk], p_ref.at[:, 2 * wk:2 * wk + gw], g_s, o_s, st_s,
                    nmat_ref, lvl_ref, hmk_ref, hmv_ref, bmt_ref)
    o_ref[...] = _head_rms_gate(o_s[...], gain_ref[...], p_ref[:, 2 * wk + gw:2 * wk + 2 * gw],
                                ones_ref[...]).astype(o_ref.dtype)


def _seq_spec(seq, width):
    return pl.BlockSpec((seq, width), lambda b: (b, 0))


def _hgrn(pa, lb, gain, batch, seq):
    consts = _gla_constants(HG_DK)
    w = HEADS * HG_DK
    return pl.pallas_call(
        _hgrn_kernel, grid=(batch,),
        in_specs=[_seq_spec(seq, PA_W), _const_spec((1, w)), _const_spec((1, GROUP_WIDTH))]
                 + [_const_spec(cst.shape) for cst in consts],
        out_specs=_seq_spec(seq, GROUP_WIDTH),
        out_shape=jax.ShapeDtypeStruct((batch * seq, GROUP_WIDTH), BF16),
        scratch_shapes=[pltpu.VMEM((seq, w), F32), pltpu.VMEM((seq, w), F32), pltpu.VMEM((seq, w), F32),
                        pltpu.VMEM((seq, GROUP_WIDTH), F32), pltpu.VMEM((GROUP_WIDTH, w), F32)],
        compiler_params=_params(("parallel",)), name="hgrn")(pa, lb, gain, *consts)


def _gla(pb, gate_w, gate_b, gain, batch, seq):
    consts = _gla_constants(GLA_DK)
    w = HEADS * GLA_DK
    return pl.pallas_call(
        _gla_kernel, grid=(batch,),
        in_specs=[_seq_spec(seq, PB_W), _const_spec((LANES, w)), _const_spec((1, w)),
                  _const_spec((1, GROUP_WIDTH))] + [_const_spec(cst.shape) for cst in consts],
        out_specs=_seq_spec(seq, GROUP_WIDTH),
        out_shape=jax.ShapeDtypeStruct((batch * seq, GROUP_WIDTH), BF16),
        scratch_shapes=[pltpu.VMEM((seq, w), F32), pltpu.VMEM((seq, w), F32),
                        pltpu.VMEM((seq, GROUP_WIDTH), F32), pltpu.VMEM((GROUP_WIDTH, w), F32)],
        compiler_params=_params(("parallel",)), name="gla")(pb, gate_w, gate_b, gain, *consts)


def _dilated_kernel(p_ref, cos_ref, sin_ref, o_ref, q_s, k_s, v_s, op_s, lse_s):
    seq, gw = o_ref.shape
    n_half = gw // LANES
    heads_per_half = LANES // DA_HD
    lane = lax.broadcasted_iota(jnp.int32, (1, LANES), 1)
    in_head = lane & (DA_HD - 1)
    first_half = in_head < (DA_ROT // 2)
    cos, sin = cos_ref[...], sin_ref[...]

    def rope(t):
        partner = jnp.where(first_half, pltpu.roll(t, LANES - DA_ROT // 2, 1),
                            pltpu.roll(t, DA_ROT // 2, 1))
        return t * cos + partner * sin

    for hf in range(n_half):
        lo = hf * LANES
        q_s[hf] = rope(p_ref[:, lo:lo + LANES]) * (DA_HD ** -0.5 * LOG2_E)
        k_s[hf] = rope(p_ref[:, gw + lo:gw + lo + LANES])
        v_s[hf] = p_ref[:, 2 * gw + lo:2 * gw + lo + LANES]

    nq = DA_STEPS
    qi = lax.broadcasted_iota(jnp.int32, (nq, 2 * nq), 0)
    kj = lax.broadcasted_iota(jnp.int32, (nq, 2 * nq), 1)
    cur_ok = (kj >= nq) & (kj - nq <= qi)
    head_of_lane = lane >> int(math.log2(DA_HD))

    for pi, (window, dil) in enumerate(DA_PATTERNS):
        n_blk = seq // (dil * nq)

        def block(i, carry, pi=pi, dil=dil, n_blk=n_blk):
            r = i // n_blk
            n = i % n_blk
            cur0 = r + n * (nq * dil)
            prev0 = r + jnp.maximum(n - 1, 0) * (nq * dil)

            def rows(start):
                if dil == 1:
                    return pl.ds(pl.multiple_of(start, nq), nq)
                return pl.ds(start, nq, stride=dil)

            first_row = qi + jnp.where(n > 0, 0, nq)
            valid = cur_ok | ((kj < nq) & (kj >= first_row))
            for hf in range(n_half):
                q = q_s[hf, rows(cur0), :]
                kwin = jnp.concatenate([k_s[hf, rows(prev0), :], k_s[hf, rows(cur0), :]], axis=0).astype(BF16)
                vwin = jnp.concatenate([v_s[hf, rows(prev0), :], v_s[hf, rows(cur0), :]], axis=0).astype(BF16)
                o_acc = jnp.zeros((nq, LANES), F32)
                lse_acc = jnp.zeros((nq, LANES), F32)
                for hd in range(heads_per_half):
                    hm = head_of_lane == hd
                    s = _dot_t(jnp.where(hm, q, 0.0).astype(BF16), kwin)
                    s = jnp.where(valid, s, MASK_VALUE)
                    m = jnp.max(s, axis=-1, keepdims=True)
                    p = jnp.exp2(s - m)
                    l = jnp.sum(p, axis=-1, keepdims=True)
                    oh = _dot(p.astype(BF16), vwin) / l
                    o_acc = jnp.where(hm, oh, o_acc)
                    lse_acc = jnp.where(hm, m * (1.0 / LOG2_E) + jnp.log(l), lse_acc)
                op_s[pi, hf, rows(cur0), :] = o_acc
                lse_s[pi, hf, rows(cur0), :] = lse_acc
            return carry

        lax.fori_loop(0, dil * n_blk, block, 0, unroll=DA_UNROLL)

    for hf in range(n_half):
        lses = [lse_s[pi, hf] for pi in range(len(DA_PATTERNS))]
        mx = functools.reduce(jnp.maximum, lses)
        ws = [jnp.exp(l - mx) for l in lses]
        num = sum(wt * op_s[pi, hf] for pi, wt in enumerate(ws))
        o_ref[:, hf * LANES:(hf + 1) * LANES] = (num / sum(ws)).astype(o_ref.dtype)


def _dilated(pc, cos_t, sin_t, batch, seq):
    gw = GROUP_WIDTH
    n_pat = len(DA_PATTERNS)
    n_half = gw // LANES
    return pl.pallas_call(
        _dilated_kernel, grid=(batch,),
        in_specs=[_seq_spec(seq, PC_W), _seq_spec(seq, LANES), _seq_spec(seq, LANES)],
        out_specs=_seq_spec(seq, gw),
        out_shape=jax.ShapeDtypeStruct((batch * seq, gw), BF16),
        scratch_shapes=[pltpu.VMEM((n_half, seq, LANES), F32)] * 3
                       + [pltpu.VMEM((n_pat, n_half, seq, LANES), F32)] * 2,
        compiler_params=_params(("parallel",)), name="dilated")(pc, cos_t, sin_t)


def _conv_kernel(p_ref, w_ref, b_ref, lg_ref, lb_ref, o_ref, u_s):
    seq, gw = o_ref.shape
    a = p_ref[:, 0:gw]
    gate = p_ref[:, gw:2 * gw]
    u_s[0:CONV_PAD, :] = jnp.zeros((CONV_PAD, gw), F32)
    u_s[CONV_PAD:CONV_PAD + seq, :] = a * jax.nn.sigmoid(gate)
    u_s[CONV_PAD + seq:CONV_PAD + seq + SUBLANES, :] = jnp.zeros((SUBLANES, gw), F32)
    w = w_ref[...]
    first = CONV_PAD - (CONV_K - 1)
    ext = CONV_TILE + SUBLANES

    def tile(ti, carry):
        t0 = pl.multiple_of(ti * CONV_TILE, CONV_TILE)
        y = jnp.zeros((CONV_TILE, gw), F32) + b_ref[...]
        for s in range(SUBLANES):
            z = None
            for a in range(s, first + CONV_K, SUBLANES):
                if a >= first:
                    term = u_s[pl.ds(pl.multiple_of(t0 + (a - s), SUBLANES), ext), :] * w[a - first:a - first + 1, :]
                    z = term if z is None else z + term
            y = y + z[s:s + CONV_TILE, :]
        mu = jnp.mean(y, axis=-1, keepdims=True)
        d = y - mu
        var = jnp.mean(d * d, axis=-1, keepdims=True)
        yn = d * lax.rsqrt(var + EPS) * lg_ref[...] + lb_ref[...]
        o_ref[pl.ds(t0, CONV_TILE), :] = (yn * jax.nn.sigmoid(yn)).astype(o_ref.dtype)
        return carry

    lax.fori_loop(0, seq // CONV_TILE, tile, 0)


def _conv(pd, w, b, ln_g, ln_b, batch, seq):
    gw = GROUP_WIDTH
    vec = _const_spec((1, gw))
    return pl.pallas_call(
        _conv_kernel, grid=(batch,),
        in_specs=[_seq_spec(seq, PD_W), _const_spec(w.shape), vec, vec, vec],
        out_specs=_seq_spec(seq, gw),
        out_shape=jax.ShapeDtypeStruct((batch * seq, gw), BF16),
        scratch_shapes=[pltpu.VMEM((CONV_PAD + seq + SUBLANES, gw), F32)],
        compiler_params=_params(("parallel",)), name="conv")(pd, w, b, ln_g, ln_b)


def _rope_tables(positions):
    half = DA_ROT // 2
    inv_freq = jnp.power(jnp.float32(ROPE_THETA), -jnp.arange(0, DA_ROT, 2, dtype=F32) / DA_ROT)
    ang = positions.astype(F32)[..., None] * inv_freq
    cos, sin = jnp.cos(ang), jnp.sin(ang)
    rest = DA_HD - DA_ROT
    ones = jnp.ones(cos.shape[:-1] + (rest,), F32)
    cos_h = jnp.concatenate([cos, cos, ones], axis=-1)
    sin_h = jnp.concatenate([-sin, sin, 0.0 * ones], axis=-1)
    b, s = positions.shape
    tile = lambda t: jnp.tile(t, (1, 1, LANES // DA_HD)).reshape(b * s, LANES)
    return tile(cos_h), tile(sin_h)


def _relayout_w_in(w_in):
    a_end = PA_W
    wk = HEADS * GLA_DK
    b_qkv = w_in[..., a_end:a_end + 2 * wk + GROUP_WIDTH]
    lr0 = a_end + 2 * wk + GROUP_WIDTH
    b_lr = w_in[..., lr0:lr0 + GLA_RANK]
    b_r = w_in[..., lr0 + GLA_RANK:lr0 + GLA_RANK + GROUP_WIDTH]
    c0 = lr0 + GLA_RANK + GROUP_WIDTH
    rest = w_in[..., c0:]
    pad = jnp.zeros(w_in.shape[:-1] + (LANES - GLA_RANK,), w_in.dtype)
    return jnp.concatenate([w_in[..., :a_end], b_qkv, b_r, b_lr, pad, rest], axis=-1).astype(BF16)


def kernel(x, mem, positions, hgrn_lb_logits, ffn1_norm, ffn1_w_up, ffn1_w_down, mix_norm, w_in, hgrn_out_norm, gla_gate_w, gla_gate_b, gla_out_norm, conv_w, conv_b, conv_ln_g, conv_ln_b, w_out, cross_norm, mem_norm, cross_wq, cross_wkv, cross_wo, ffn2_norm, ffn2_w_up, ffn2_w_down, final_norm):
    batch, seq, d = x.shape
    mem_len = mem.shape[1]
    depth = w_in.shape[0]
    assert d == D_MODEL and seq % (DA_PATTERNS[-1][1] * DA_STEPS) == 0
    assert (batch * seq) % FFN_TILE == 0 and seq % ROW_TILE == 0 and (batch * mem_len) % ROW_TILE == 0

    cos_t, sin_t = _rope_tables(positions)
    p_lb = jax.nn.softmax(hgrn_lb_logits.astype(F32), axis=0)
    lower_bounds = jnp.cumsum(p_lb, axis=0) - p_lb[0:1]

    row = lambda v: v.reshape(1, -1).astype(F32)
    x2 = x.reshape(batch * seq, d)
    mem2 = mem.reshape(batch * mem_len, d)
    up1, down1 = ffn1_w_up.astype(BF16), ffn1_w_down.astype(BF16)
    up2, down2 = ffn2_w_up.astype(BF16), ffn2_w_down.astype(BF16)
    w_in_b, w_out_b = _relayout_w_in(w_in), w_out.astype(BF16)
    wq_b, wkv_b, wo_b = cross_wq.astype(BF16), cross_wkv.astype(BF16), cross_wo.astype(BF16)
    for l in range(depth):
        x2 = _ffn(x2, row(ffn1_norm[l]), up1, down1, l)

        pa, pb, pc, pd = _inproj(x2, row(mix_norm[l]), w_in_b, l)
        gate_w = jnp.zeros((LANES, HEADS * GLA_DK), F32).at[:GLA_RANK].set(gla_gate_w[l])
        o_a = _hgrn(pa, row(lower_bounds[l]), row(hgrn_out_norm[l]), batch, seq)
        o_b = _gla(pb, gate_w, row(gla_gate_b[l]), row(gla_out_norm[l]), batch, seq)
        o_c = _dilated(pc, cos_t, sin_t, batch, seq)
        o_d = _conv(pd, conv_w[l], row(conv_b[l]), row(conv_ln_g[l]), row(conv_ln_b[l]), batch, seq)
        kv = _memkv(mem2, row(mem_norm[l]), wkv_b, l)
        x2 = _cross(x2, (o_a, o_b, o_c, o_d), w_out_b, row(cross_norm[l]), wq_b, kv, wo_b, l, seq, mem_len)

        last = l == depth - 1
        x2 = _ffn(x2, row(ffn2_norm[l]), up2, down2, l, final_g=row(final_norm) if last else None)
    return x2.reshape(batch, seq, d)
```

```python
import functools
import math

import numpy as np
import jax
import jax.numpy as jnp
from jax import lax
from jax.experimental import pallas as pl
from jax.experimental.pallas import tpu as pltpu

F32 = jnp.float32
BF16 = jnp.bfloat16

D_MODEL = 1024
GROUP_WIDTH = D_MODEL // 4
HEADS = 4
HG_DK = GROUP_WIDTH // HEADS
GLA_DK = GROUP_WIDTH // (2 * HEADS)
HEAD_DV = GROUP_WIDTH // HEADS
GLA_RANK = 16
GLA_TAU = 16.0
HG_LB_FLOOR = 1e-20
DA_HD = GROUP_WIDTH // HEADS
DA_ROT = DA_HD // 4
ROPE_THETA = 500000.0
DA_PATTERNS = ((128, 1), (512, 4), (2048, 16))
MASK_VALUE = -1e30
LOG2_E = math.log2(math.e)
CONV_K = 31
X_HD = D_MODEL // HEADS
D_FF = ((int(8 * D_MODEL / 3) + 255) // 256) * 256
EPS = 1e-6

LANES = 128
SUBLANES = 8
ROW_TILE = 1024
FFN_TILE = 1024
MXU_DIM = 256
FF_CHUNK = 4 * MXU_DIM
GLA_CHUNK = 64
GLA_UNROLL = 16
DA_STEPS = 128
DA_UNROLL = 8
CONV_TILE = 128
CONV_PAD = 32
VMEM_LIMIT = 56 * 1024 * 1024

PA_W = 4 * GROUP_WIDTH
PB_W = 2 * HEADS * GLA_DK + 2 * GROUP_WIDTH + LANES
PC_W = 3 * GROUP_WIDTH
PD_W = 2 * GROUP_WIDTH


def _dot(a, b):
    return jnp.dot(a, b, preferred_element_type=F32)


def _dot_t(a, b):
    return lax.dot_general(a, b, (((1,), (1,)), ((), ())), preferred_element_type=F32)


def _tdot(a, b):
    return lax.dot_general(a, b, (((0,), (0,)), ((), ())), preferred_element_type=F32)


def _rms(x, g):
    ms = jnp.mean(x * x, axis=-1, keepdims=True)
    return x * lax.rsqrt(ms + EPS) * g


def _split3(x):
    hi = x.astype(BF16)
    r1 = x - hi.astype(F32)
    mid = r1.astype(BF16)
    lo = (r1 - mid.astype(F32)).astype(BF16)
    return hi, mid, lo


def _const_spec(shape):
    nd = len(shape)
    return pl.BlockSpec(shape, lambda *_: (0,) * nd, pipeline_mode=pl.Buffered(1))


def _layer_spec(stacked, layer):
    nd = stacked.ndim - 1
    return pl.BlockSpec((None,) + stacked.shape[1:], lambda *_: (layer,) + (0,) * nd,
                        pipeline_mode=pl.Buffered(1))


def _params(sem):
    return pltpu.CompilerParams(dimension_semantics=sem, vmem_limit_bytes=VMEM_LIMIT)


def _ffn_body(x_ref, g_ref, wup_ref, wd_ref):
    x = x_ref[...]
    h = _rms(x, g_ref[...]).astype(BF16)
    acc = jnp.zeros_like(x)
    for lo in range(0, D_FF, FF_CHUNK):
        width = min(FF_CHUNK, D_FF - lo)
        gate = _dot(h, wup_ref[:, lo:lo + width])
        up = _dot(h, wup_ref[:, D_FF + lo:D_FF + lo + width])
        a = (gate * jax.nn.sigmoid(gate) * up).astype(BF16)
        acc = acc + _dot(a, wd_ref[lo:lo + width, :])
    return x + 0.5 * acc


def _ffn_kernel(x_ref, g_ref, wup_ref, wd_ref, o_ref):
    o_ref[...] = _ffn_body(x_ref, g_ref, wup_ref, wd_ref)


def _ffn_final_kernel(x_ref, g_ref, wup_ref, wd_ref, fg_ref, o_ref):
    o_ref[...] = _rms(_ffn_body(x_ref, g_ref, wup_ref, wd_ref), fg_ref[...])


def _ffn(x2, g, wup, wd, layer, final_g=None):
    n = x2.shape[0]
    row = pl.BlockSpec((FFN_TILE, D_MODEL), lambda i: (i, 0))
    in_specs = [row, _const_spec((1, D_MODEL)), _layer_spec(wup, layer), _layer_spec(wd, layer)]
    args = [x2, g, wup, wd]
    body = _ffn_kernel
    if final_g is not None:
        in_specs.append(_const_spec((1, D_MODEL)))
        args.append(final_g)
        body = _ffn_final_kernel
    return pl.pallas_call(
        body, grid=(n // FFN_TILE,), in_specs=in_specs, out_specs=row,
        out_shape=jax.ShapeDtypeStruct(x2.shape, F32),
        compiler_params=_params(("parallel",)), name="ffn")(*args)


def _inproj_kernel(x_ref, g_ref, w_ref, pa_ref, pb_ref, pc_ref, pd_ref):
    h = _rms(x_ref[...], g_ref[...]).astype(BF16)
    off = 0
    for ref, width in ((pa_ref, PA_W), (pb_ref, PB_W), (pc_ref, PC_W), (pd_ref, PD_W)):
        ref[...] = _dot(h, w_ref[:, off:off + width])
        off += width


def _inproj(x2, g, w, layer):
    n = x2.shape[0]
    widths = (PA_W, PB_W, PC_W, PD_W)
    return pl.pallas_call(
        _inproj_kernel, grid=(n // ROW_TILE,),
        in_specs=[pl.BlockSpec((ROW_TILE, D_MODEL), lambda i: (i, 0)),
                  _const_spec((1, D_MODEL)), _layer_spec(w, layer)],
        out_specs=[pl.BlockSpec((ROW_TILE, wd), lambda i: (i, 0)) for wd in widths],
        out_shape=[jax.ShapeDtypeStruct((n, wd), F32) for wd in widths],
        compiler_params=_params(("parallel",)), name="inproj")(x2, g, w)


def _memkv_kernel(m_ref, g_ref, w_ref, kv_ref):
    h = _rms(m_ref[...], g_ref[...]).astype(BF16)
    kv_ref[...] = _dot(h, w_ref[...]).astype(BF16)


def _memkv(mem2, g, w, layer):
    n = mem2.shape[0]
    return pl.pallas_call(
        _memkv_kernel, grid=(n // ROW_TILE,),
        in_specs=[pl.BlockSpec((ROW_TILE, D_MODEL), lambda i: (i, 0)),
                  _const_spec((1, D_MODEL)), _layer_spec(w, layer)],
        out_specs=pl.BlockSpec((ROW_TILE, 2 * D_MODEL), lambda i: (i, 0)),
        out_shape=jax.ShapeDtypeStruct((n, 2 * D_MODEL), BF16),
        compiler_params=_params(("parallel",)), name="memkv")(mem2, g, w)


def _cross_kernel(x_ref, oa_ref, ob_ref, oc_ref, od_ref, wout_ref, g_ref, wq_ref, kv_ref, wo_ref, o_ref):
    x = x_ref[...]
    for j, ref in enumerate((oa_ref, ob_ref, oc_ref, od_ref)):
        x = x + _dot(ref[...], wout_ref[j * GROUP_WIDTH:(j + 1) * GROUP_WIDTH, :])
    h = _rms(x, g_ref[...]).astype(BF16)
    q = _dot(h, wq_ref[...]).astype(BF16)
    heads = []
    for hd in range(HEADS):
        lo = hd * X_HD
        s = _dot_t(q[:, lo:lo + X_HD], kv_ref[:, lo:lo + X_HD]) * (X_HD ** -0.5)
        s = s - jnp.max(s, axis=-1, keepdims=True)
        p = jnp.exp(s)
        p = p / jnp.sum(p, axis=-1, keepdims=True)
        heads.append(_dot(p.astype(BF16), kv_ref[:, D_MODEL + lo:D_MODEL + lo + X_HD]).astype(BF16))
    o = jnp.concatenate(heads, axis=-1)
    o_ref[...] = x + _dot(o, wo_ref[...])


def _cross(x2, outs, w_out, g, wq, kv, wo, layer, seq, mem_len):
    n = x2.shape[0]
    tiles_per_seq = seq // ROW_TILE
    row = pl.BlockSpec((ROW_TILE, D_MODEL), lambda i: (i, 0))
    grp = pl.BlockSpec((ROW_TILE, GROUP_WIDTH), lambda i: (i, 0))
    return pl.pallas_call(
        _cross_kernel, grid=(n // ROW_TILE,),
        in_specs=[row, grp, grp, grp, grp, _layer_spec(w_out, layer),
                  _const_spec((1, D_MODEL)), _layer_spec(wq, layer),
                  pl.BlockSpec((mem_len, 2 * D_MODEL), lambda i: (i // tiles_per_seq, 0)),
                  _layer_spec(wo, layer)],
        out_specs=row, out_shape=jax.ShapeDtypeStruct(x2.shape, F32),
        compiler_params=_params(("parallel",)), name="cross")(x2, *outs, w_out, g, wq, kv, wo)


def _gla_constants(dk):
    c, h = GLA_CHUNK, HEADS
    w, wv = h * dk, h * HEAD_DV
    r = np.arange(c)
    blocks = [(r[:, None] >= r[None, :])]
    masks = [(r[:, None] == r[None, :])]
    m = 1
    while m < c:
        ref = (r // (2 * m)) * (2 * m) + m - 1
        upper = (r % (2 * m)) >= m
        rp = r[None, :]
        rng_up = (rp > ref[:, None]) & (rp <= r[:, None])
        rng_lo = (rp > r[:, None]) & (rp <= ref[:, None])
        blocks.append(np.where(upper[:, None], rng_up, rng_lo))
        same = (r[:, None] // (2 * m)) == (r[None, :] // (2 * m))
        masks.append(same & upper[:, None] & (~upper)[None, :])
        m *= 2
    nmat = np.concatenate(blocks, axis=0).astype(np.float32)
    lvl = np.stack([np.tile(mk, (1, h)) for mk in masks]).astype(np.float32)
    rows = np.arange(h * c)[:, None] // c
    hm_k = (rows == (np.arange(w)[None, :] // dk)).astype(np.float32)
    hm_v = (rows == (np.arange(wv)[None, :] // HEAD_DV)).astype(np.float32)
    bm_t = ((np.arange(wv)[:, None] // HEAD_DV) == (np.arange(w)[None, :] // dk)).astype(np.float32)
    ones_blk = ((np.arange(wv)[:, None] // HEAD_DV) == (np.arange(wv)[None, :] // HEAD_DV))
    return (jnp.asarray(nmat, BF16), jnp.asarray(lvl, F32), jnp.asarray(hm_k, BF16),
            jnp.asarray(hm_v, BF16), jnp.asarray(bm_t, F32), jnp.asarray(ones_blk.astype(np.float32), BF16))


def _gla_recurrence(seq, q_ref, k_ref, v_ref, g_ref, o_ref, st_ref,
                    nmat_ref, lvl_ref, hmk_ref, hmv_ref, bmt_ref):
    c, h = GLA_CHUNK, HEADS
    n_lvl = lvl_ref.shape[0]
    st_ref[...] = jnp.zeros_like(st_ref)

    def step(ci, carry):
        r0 = pl.multiple_of(ci * c, c)
        q = q_ref[pl.ds(r0, c), :]
        k = k_ref[pl.ds(r0, c), :]
        v = v_ref[pl.ds(r0, c), :]
        g = g_ref[pl.ds(r0, c), :]
        g_hi = g.astype(BF16)
        g_mid = (g - g_hi.astype(F32)).astype(BF16)
        cum = nmat_ref[0:c, :]
        b = _dot(cum, g_hi) + _dot(cum, g_mid)
        lvl_rows = nmat_ref[c:, :]
        sums = _dot(lvl_rows, g_hi)
        hmk = hmk_ref[...]
        acc = jnp.zeros((c, h * c), F32)
        for li in range(n_lvl):
            if li == 0:
                qt, kt = q, k
            else:
                e = jnp.exp(sums[(li - 1) * c:li * c])
                qt, kt = q * e, k * e
            kst = jnp.concatenate([kt.astype(BF16)] * h, axis=0) * hmk
            acc = acc + _dot_t(qt.astype(BF16), kst) * lvl_ref[li]
        vb = v.astype(BF16)
        vst = jnp.concatenate([vb] * h, axis=0) * hmv_ref[...]
        o = _dot(acc.astype(BF16), vst)
        st = st_ref[...]
        o = o + _dot_t((q * jnp.exp(b)).astype(BF16), st.astype(BF16))
        b_last = b[c - 1:c, :]
        kh = (k * jnp.exp(b_last - b)).astype(BF16)
        st_ref[...] = st * jnp.exp(b_last) + _tdot(vb, kh) * bmt_ref[...]
        o_ref[pl.ds(r0, c), :] = o
        return carry

    lax.fori_loop(0, seq // c, step, 0, unroll=GLA_UNROLL)


def _head_rms_gate(o, gain, gate, ones_blk):
    sq = o * o
    hi = sq.astype(BF16)
    lo = (sq - hi.astype(F32)).astype(BF16)
    ms = (_dot(hi, ones_blk) + _dot(lo, ones_blk)) * (1.0 / HEAD_DV)
    return o * lax.rsqrt(ms + EPS) * gain * (gate * jax.nn.sigmoid(gate))


def _hgrn_kernel(p_ref, lb_ref, gain_ref, nmat_ref, lvl_ref, hmk_ref, hmv_ref, bmt_ref, ones_ref,
                 o_ref, q_s, k_s, g_s, o_s, st_s):
    seq, gw = o_ref.shape
    z = p_ref[:, gw:2 * gw]
    lb = lb_ref[...]
    lbf = jnp.maximum(lb, HG_LB_FLOOR)
    e = jnp.exp(-jnp.abs(z))
    inv = 1.0 / (1.0 + e)
    sig = jnp.where(z >= 0, inv, e * inv)
    nsig = jnp.where(z >= 0, e * inv, inv)
    f = lbf + (1.0 - lb) * sig
    g_s[...] = jnp.log(f)
    k_s[...] = (1.0 - lb) * nsig - (lbf - lb)
    q_s[...] = p_ref[:, 0:gw]
    _gla_recurrence(seq, q_s, k_s, p_ref.at[:, 2 * gw:3 * gw], g_s, o_s, st_s,
                    nmat_ref, lvl_ref, hmk_ref, hmv_ref, bmt_ref)
    o_ref[...] = _head_rms_gate(o_s[...], gain_ref[...], p_ref[:, 3 * gw:4 * gw],
                                ones_ref[...]).astype(o_ref.dtype)


def _gla_kernel(p_ref, gw_ref, gb_ref, gain_ref, nmat_ref, lvl_ref, hmk_ref, hmv_ref, bmt_ref,
                ones_ref, o_ref, q_s, g_s, o_s, st_s):
    seq, gw = o_ref.shape
    wk = HEADS * GLA_DK
    lr = p_ref[:, 2 * wk + 2 * gw:2 * wk + 2 * gw + LANES]
    lr_hi = lr.astype(BF16)
    lr_lo = (lr - lr_hi.astype(F32)).astype(BF16)
    w = gw_ref[...]
    w_hi = w.astype(BF16)
    w_lo = (w - w_hi.astype(F32)).astype(BF16)
    y = _dot(lr_hi, w_hi) + _dot(lr_hi, w_lo) + _dot(lr_lo, w_hi) + gb_ref[...]
    g_s[...] = (jnp.minimum(y, 0.0) - jnp.log(1.0 + jnp.exp(-jnp.abs(y)))) * (1.0 / GLA_TAU)
    q_s[...] = p_ref[:, 0:wk] * (GLA_DK ** -0.5)
    _gla_recurrence(seq, q_s, p_ref.at[:, wk:2 * wk], p_ref.at[:, 2 * wk:2 * wk + gw], g_s, o_s, st_s,
                    nmat_ref, lvl_ref, hmk_ref, hmv_ref, bmt_ref)
    o_ref[...] = _head_rms_gate(o_s[...], gain_ref[...], p_ref[:, 2 * wk + gw:2 * wk + 2 * gw],
                                ones_ref[...]).astype(o_ref.dtype)


def _seq_spec(seq, width):
    return pl.BlockSpec((seq, width), lambda b: (b, 0))


def _hgrn(pa, lb, gain, batch, seq):
    consts = _gla_constants(HG_DK)
    w = HEADS * HG_DK
    return pl.pallas_call(
        _hgrn_kernel, grid=(batch,),
        in_specs=[_seq_spec(seq, PA_W), _const_spec((1, w)), _const_spec((1, GROUP_WIDTH))]
                 + [_const_spec(cst.shape) for cst in consts],
        out_specs=_seq_spec(seq, GROUP_WIDTH),
        out_shape=jax.ShapeDtypeStruct((batch * seq, GROUP_WIDTH), BF16),
        scratch_shapes=[pltpu.VMEM((seq, w), F32), pltpu.VMEM((seq, w), F32), pltpu.VMEM((seq, w), F32),
                        pltpu.VMEM((seq, GROUP_WIDTH), F32), pltpu.VMEM((GROUP_WIDTH, w), F32)],
        compiler_params=_params(("parallel",)), name="hgrn")(pa, lb, gain, *consts)


def _gla(pb, gate_w, gate_b, gain, batch, seq):
    consts = _gla_constants(GLA_DK)
    w = HEADS * GLA_DK
    return pl.pallas_call(
        _gla_kernel, grid=(batch,),
        in_specs=[_seq_spec(seq, PB_W), _const_spec((LANES, w)), _const_spec((1, w)),
                  _const_spec((1, GROUP_WIDTH))] + [_const_spec(cst.shape) for cst in consts],
        out_specs=_seq_spec(seq, GROUP_WIDTH),
        out_shape=jax.ShapeDtypeStruct((batch * seq, GROUP_WIDTH), BF16),
        scratch_shapes=[pltpu.VMEM((seq, w), F32), pltpu.VMEM((seq, w), F32),
                        pltpu.VMEM((seq, GROUP_WIDTH), F32), pltpu.VMEM((GROUP_WIDTH, w), F32)],
        compiler_params=_params(("parallel",)), name="gla")(pb, gate_w, gate_b, gain, *consts)


def _dilated_kernel(p_ref, cos_ref, sin_ref, o_ref, q_s, k_s, v_s, op_s, lse_s):
    seq, gw = o_ref.shape
    n_half = gw // LANES
    heads_per_half = LANES // DA_HD
    lane = lax.broadcasted_iota(jnp.int32, (1, LANES), 1)
    in_head = lane & (DA_HD - 1)
    first_half = in_head < (DA_ROT // 2)
    cos, sin = cos_ref[...], sin_ref[...]

    def rope(t):
        partner = jnp.where(first_half, pltpu.roll(t, LANES - DA_ROT // 2, 1),
                            pltpu.roll(t, DA_ROT // 2, 1))
        return t * cos + partner * sin

    for hf in range(n_half):
        lo = hf * LANES
        q_s[hf] = rope(p_ref[:, lo:lo + LANES]) * (DA_HD ** -0.5 * LOG2_E)
        k_s[hf] = rope(p_ref[:, gw + lo:gw + lo + LANES])
        v_s[hf] = p_ref[:, 2 * gw + lo:2 * gw + lo + LANES]

    nq = DA_STEPS
    qi = lax.broadcasted_iota(jnp.int32, (nq, 2 * nq), 0)
    kj = lax.broadcasted_iota(jnp.int32, (nq, 2 * nq), 1)
    cur_ok = (kj >= nq) & (kj - nq <= qi)
    head_of_lane = lane >> int(math.log2(DA_HD))

    for pi, (window, dil) in enumerate(DA_PATTERNS):
        n_blk = seq // (dil * nq)

        def block(i, carry, pi=pi, dil=dil, n_blk=n_blk):
            r = i // n_blk
            n = i % n_blk
            cur0 = r + n * (nq * dil)
            prev0 = r + jnp.maximum(n - 1, 0) * (nq * dil)

            def rows(start):
                if dil == 1:
                    return pl.ds(pl.multiple_of(start, nq), nq)
                return pl.ds(start, nq, stride=dil)

            first_row = qi + jnp.where(n > 0, 0, nq)
            valid = cur_ok | ((kj < nq) & (kj >= first_row))
            for hf in range(n_half):
                q = q_s[hf, rows(cur0), :]
                kwin = jnp.concatenate([k_s[hf, rows(prev0), :], k_s[hf, rows(cur0), :]], axis=0).astype(BF16)
                vwin = jnp.concatenate([v_s[hf, rows(prev0), :], v_s[hf, rows(cur0), :]], axis=0).astype(BF16)
                o_acc = jnp.zeros((nq, LANES), F32)
                lse_acc = jnp.zeros((nq, LANES), F32)
                for hd in range(heads_per_half):
                    hm = head_of_lane == hd
                    s = _dot_t(jnp.where(hm, q, 0.0).astype(BF16), kwin)
                    s = jnp.where(valid, s, MASK_VALUE)
                    m = jnp.max(s, axis=-1, keepdims=True)
                    p = jnp.exp2(s - m)
                    l = jnp.sum(p, axis=-1, keepdims=True)
                    oh = _dot(p.astype(BF16), vwin) / l
                    o_acc = jnp.where(hm, oh, o_acc)
                    lse_acc = jnp.where(hm, m * (1.0 / LOG2_E) + jnp.log(l), lse_acc)
                op_s[pi, hf, rows(cur0), :] = o_acc
                lse_s[pi, hf, rows(cur0), :] = lse_acc
            return carry

        lax.fori_loop(0, dil * n_blk, block, 0, unroll=DA_UNROLL)

    for hf in range(n_half):
        lses = [lse_s[pi, hf] for pi in range(len(DA_PATTERNS))]
        mx = functools.reduce(jnp.maximum, lses)
        ws = [jnp.exp(l - mx) for l in lses]
        num = sum(wt * op_s[pi, hf] for pi, wt in enumerate(ws))
        o_ref[:, hf * LANES:(hf + 1) * LANES] = (num / sum(ws)).astype(o_ref.dtype)


def _dilated(pc, cos_t, sin_t, batch, seq):
    gw = GROUP_WIDTH
    n_pat = len(DA_PATTERNS)
    n_half = gw // LANES
    return pl.pallas_call(
        _dilated_kernel, grid=(batch,),
        in_specs=[_seq_spec(seq, PC_W), _seq_spec(seq, LANES), _seq_spec(seq, LANES)],
        out_specs=_seq_spec(seq, gw),
        out_shape=jax.ShapeDtypeStruct((batch * seq, gw), BF16),
        scratch_shapes=[pltpu.VMEM((n_half, seq, LANES), F32)] * 3
                       + [pltpu.VMEM((n_pat, n_half, seq, LANES), F32)] * 2,
        compiler_params=_params(("parallel",)), name="dilated")(pc, cos_t, sin_t)


def _conv_kernel(p_ref, w_ref, b_ref, lg_ref, lb_ref, o_ref, u_s):
    seq, gw = o_ref.shape
    a = p_ref[:, 0:gw]
    gate = p_ref[:, gw:2 * gw]
    u_s[0:CONV_PAD, :] = jnp.zeros((CONV_PAD, gw), F32)
    u_s[CONV_PAD:CONV_PAD + seq, :] = a * jax.nn.sigmoid(gate)
    u_s[CONV_PAD + seq:CONV_PAD + seq + SUBLANES, :] = jnp.zeros((SUBLANES, gw), F32)
    w = w_ref[...]
    first = CONV_PAD - (CONV_K - 1)
    ext = CONV_TILE + SUBLANES

    def tile(ti, carry):
        t0 = pl.multiple_of(ti * CONV_TILE, CONV_TILE)
        y = jnp.zeros((CONV_TILE, gw), F32) + b_ref[...]
        for s in range(SUBLANES):
            z = None
            for a in range(s, first + CONV_K, SUBLANES):
                if a >= first:
                    term = u_s[pl.ds(pl.multiple_of(t0 + (a - s), SUBLANES), ext), :] * w[a - first:a - first + 1, :]
                    z = term if z is None else z + term
            y = y + z[s:s + CONV_TILE, :]
        mu = jnp.mean(y, axis=-1, keepdims=True)
        d = y - mu
        var = jnp.mean(d * d, axis=-1, keepdims=True)
        yn = d * lax.rsqrt(var + EPS) * lg_ref[...] + lb_ref[...]
        o_ref[pl.ds(t0, CONV_TILE), :] = (yn * jax.nn.sigmoid(yn)).astype(o_ref.dtype)
        return carry

    lax.fori_loop(0, seq // CONV_TILE, tile, 0)


def _conv(pd, w, b, ln_g, ln_b, batch, seq):
    gw = GROUP_WIDTH
    vec = _const_spec((1, gw))
    return pl.pallas_call(
        _conv_kernel, grid=(batch,),
        in_specs=[_seq_spec(seq, PD_W), _const_spec(w.shape), vec, vec, vec],
        out_specs=_seq_spec(seq, gw),
        out_shape=jax.ShapeDtypeStruct((batch * seq, gw), BF16),
        scratch_shapes=[pltpu.VMEM((CONV_PAD + seq + SUBLANES, gw), F32)],
        compiler_params=_params(("parallel",)), name="conv")(pd, w, b, ln_g, ln_b)


def _rope_tables(positions):
    half = DA_ROT // 2
    inv_freq = jnp.power(jnp.float32(ROPE_THETA), -jnp.arange(0, DA_ROT, 2, dtype=F32) / DA_ROT)
    ang = positions.astype(F32)[..., None] * inv_freq
    cos, sin = jnp.cos(ang), jnp.sin(ang)
    rest = DA_HD - DA_ROT
    ones = jnp.ones(cos.shape[:-1] + (rest,), F32)
    cos_h = jnp.concatenate([cos, cos, ones], axis=-1)
    sin_h = jnp.concatenate([-sin, sin, 0.0 * ones], axis=-1)
    b, s = positions.shape
    tile = lambda t: jnp.tile(t, (1, 1, LANES // DA_HD)).reshape(b * s, LANES)
    return tile(cos_h), tile(sin_h)


def _relayout_w_in(w_in):
    a_end = PA_W
    wk = HEADS * GLA_DK
    b_qkv = w_in[..., a_end:a_end + 2 * wk + GROUP_WIDTH]
    lr0 = a_end + 2 * wk + GROUP_WIDTH
    b_lr = w_in[..., lr0:lr0 + GLA_RANK]
    b_r = w_in[..., lr0 + GLA_RANK:lr0 + GLA_RANK + GROUP_WIDTH]
    c0 = lr0 + GLA_RANK + GROUP_WIDTH
    rest = w_in[..., c0:]
    pad = jnp.zeros(w_in.shape[:-1] + (LANES - GLA_RANK,), w_in.dtype)
    return jnp.concatenate([w_in[..., :a_end], b_qkv, b_r, b_lr, pad, rest], axis=-1).astype(BF16)


def kernel(x, mem, positions, hgrn_lb_logits, ffn1_norm, ffn1_w_up, ffn1_w_down, mix_norm, w_in, hgrn_out_norm, gla_gate_w, gla_gate_b, gla_out_norm, conv_w, conv_b, conv_ln_g, conv_ln_b, w_out, cross_norm, mem_norm, cross_wq, cross_wkv, cross_wo, ffn2_norm, ffn2_w_up, ffn2_w_down, final_norm):
    batch, seq, d = x.shape
    mem_len = mem.shape[1]
    depth = w_in.shape[0]
    assert d == D_MODEL and seq % (DA_PATTERNS[-1][1] * DA_STEPS) == 0
    assert (batch * seq) % FFN_TILE == 0 and seq % ROW_TILE == 0 and (batch * mem_len) % ROW_TILE == 0

    cos_t, sin_t = _rope_tables(positions)
    p_lb = jax.nn.softmax(hgrn_lb_logits.astype(F32), axis=0)
    lower_bounds = jnp.cumsum(p_lb, axis=0) - p_lb[0:1]

    row = lambda v: v.reshape(1, -1).astype(F32)
    x2 = x.reshape(batch * seq, d)
    mem2 = mem.reshape(batch * mem_len, d)
    up1, down1 = ffn1_w_up.astype(BF16), ffn1_w_down.astype(BF16)
    up2, down2 = ffn2_w_up.astype(BF16), ffn2_w_down.astype(BF16)
    w_in_b, w_out_b = _relayout_w_in(w_in), w_out.astype(BF16)
    wq_b, wkv_b, wo_b = cross_wq.astype(BF16), cross_wkv.astype(BF16), cross_wo.astype(BF16)
    for l in range(depth):
        x2 = _ffn(x2, row(ffn1_norm[l]), up1, down1, l)

        pa, pb, pc, pd = _inproj(x2, row(mix_norm[l]), w_in_b, l)
        gate_w = jnp.zeros((LANES, HEADS * GLA_DK), F32).at[:GLA_RANK].set(gla_gate_w[l])
        o_a = _hgrn(pa, row(lower_bounds[l]), row(hgrn_out_norm[l]), batch, seq)
        o_b = _gla(pb, gate_w, row(gla_gate_b[l]), row(gla_out_norm[l]), batch, seq)
        o_c = _dilated(pc, cos_t, sin_t, batch, seq)
        o_d = _conv(pd, conv_w[l], row(conv_b[l]), row(conv_ln_g[l]), row(conv_ln_b[l]), batch, seq)
        kv = _memkv(mem2, row(mem_norm[l]), wkv_b, l)
        x2 = _cross(x2, (o_a, o_b, o_c, o_d), w_out_b, row(cross_norm[l]), wq_b, kv, wo_b, l, seq, mem_len)

        last = l == depth - 1
        x2 = _ffn(x2, row(ffn2_norm[l]), up2, down2, l, final_g=row(final_norm) if last else None)
    return x2.reshape(batch, seq, d)
```

```python
import functools
import math

import numpy as np
import jax
import jax.numpy as jnp
from jax import lax
from jax.experimental import pallas as pl
from jax.experimental.pallas import tpu as pltpu

F32 = jnp.float32
BF16 = jnp.bfloat16

D_MODEL = 1024
GROUP_WIDTH = D_MODEL // 4
HEADS = 4
HG_DK = GROUP_WIDTH // HEADS
GLA_DK = GROUP_WIDTH // (2 * HEADS)
HEAD_DV = GROUP_WIDTH // HEADS
GLA_RANK = 16
GLA_TAU = 16.0
HG_LB_FLOOR = 1e-20
DA_HD = GROUP_WIDTH // HEADS
DA_ROT = DA_HD // 4
ROPE_THETA = 500000.0
DA_PATTERNS = ((128, 1), (512, 4), (2048, 16))
MASK_VALUE = -1e30
LOG2_E = math.log2(math.e)
CONV_K = 31
X_HD = D_MODEL // HEADS
D_FF = ((int(8 * D_MODEL / 3) + 255) // 256) * 256
EPS = 1e-6

LANES = 128
SUBLANES = 8
ROW_TILE = 1024
FFN_TILE = 1024
MXU_DIM = 256
FF_CHUNK = 4 * MXU_DIM
GLA_CHUNK = 64
GLA_UNROLL = 16
DA_STEPS = 128
DA_UNROLL = 8
CONV_TILE = 128
CONV_UNROLL = 4
CONV_PAD = 32
VMEM_LIMIT = 56 * 1024 * 1024

PA_W = 4 * GROUP_WIDTH
PB_W = 2 * HEADS * GLA_DK + 2 * GROUP_WIDTH + LANES
PC_W = 3 * GROUP_WIDTH
PD_W = 2 * GROUP_WIDTH


def _dot(a, b):
    return jnp.dot(a, b, preferred_element_type=F32)


def _dot_t(a, b):
    return lax.dot_general(a, b, (((1,), (1,)), ((), ())), preferred_element_type=F32)


def _tdot(a, b):
    return lax.dot_general(a, b, (((0,), (0,)), ((), ())), preferred_element_type=F32)


def _rms(x, g):
    ms = jnp.mean(x * x, axis=-1, keepdims=True)
    return x * lax.rsqrt(ms + EPS) * g


def _split3(x):
    hi = x.astype(BF16)
    r1 = x - hi.astype(F32)
    mid = r1.astype(BF16)
    lo = (r1 - mid.astype(F32)).astype(BF16)
    return hi, mid, lo


def _const_spec(shape):
    nd = len(shape)
    return pl.BlockSpec(shape, lambda *_: (0,) * nd, pipeline_mode=pl.Buffered(1))


def _layer_spec(stacked, layer):
    nd = stacked.ndim - 1
    return pl.BlockSpec((None,) + stacked.shape[1:], lambda *_: (layer,) + (0,) * nd,
                        pipeline_mode=pl.Buffered(1))


def _params(sem):
    return pltpu.CompilerParams(dimension_semantics=sem, vmem_limit_bytes=VMEM_LIMIT)


def _ffn_body(x_ref, g_ref, wup_ref, wd_ref):
    x = x_ref[...]
    h = _rms(x, g_ref[...]).astype(BF16)
    acc = jnp.zeros_like(x)
    for lo in range(0, D_FF, FF_CHUNK):
        width = min(FF_CHUNK, D_FF - lo)
        gate = _dot(h, wup_ref[:, lo:lo + width])
        up = _dot(h, wup_ref[:, D_FF + lo:D_FF + lo + width])
        a = (gate * jax.nn.sigmoid(gate) * up).astype(BF16)
        acc = acc + _dot(a, wd_ref[lo:lo + width, :])
    return x + 0.5 * acc


def _ffn_kernel(x_ref, g_ref, wup_ref, wd_ref, o_ref):
    o_ref[...] = _ffn_body(x_ref, g_ref, wup_ref, wd_ref)


def _ffn_final_kernel(x_ref, g_ref, wup_ref, wd_ref, fg_ref, o_ref):
    o_ref[...] = _rms(_ffn_body(x_ref, g_ref, wup_ref, wd_ref), fg_ref[...])


def _ffn(x2, g, wup, wd, layer, final_g=None):
    n = x2.shape[0]
    row = pl.BlockSpec((FFN_TILE, D_MODEL), lambda i: (i, 0))
    in_specs = [row, _const_spec((1, D_MODEL)), _layer_spec(wup, layer), _layer_spec(wd, layer)]
    args = [x2, g, wup, wd]
    body = _ffn_kernel
    if final_g is not None:
        in_specs.append(_const_spec((1, D_MODEL)))
        args.append(final_g)
        body = _ffn_final_kernel
    return pl.pallas_call(
        body, grid=(n // FFN_TILE,), in_specs=in_specs, out_specs=row,
        out_shape=jax.ShapeDtypeStruct(x2.shape, F32),
        compiler_params=_params(("parallel",)), name="ffn")(*args)


def _inproj_kernel(x_ref, g_ref, w_ref, pa_ref, pb_ref, pc_ref, pd_ref):
    h = _rms(x_ref[...], g_ref[...]).astype(BF16)
    off = 0
    for ref, width in ((pa_ref, PA_W), (pb_ref, PB_W), (pc_ref, PC_W), (pd_ref, PD_W)):
        ref[...] = _dot(h, w_ref[:, off:off + width])
        off += width


def _inproj(x2, g, w, layer):
    n = x2.shape[0]
    widths = (PA_W, PB_W, PC_W, PD_W)
    return pl.pallas_call(
        _inproj_kernel, grid=(n // ROW_TILE,),
        in_specs=[pl.BlockSpec((ROW_TILE, D_MODEL), lambda i: (i, 0)),
                  _const_spec((1, D_MODEL)), _layer_spec(w, layer)],
        out_specs=[pl.BlockSpec((ROW_TILE, wd), lambda i: (i, 0)) for wd in widths],
        out_shape=[jax.ShapeDtypeStruct((n, wd), F32) for wd in widths],
        compiler_params=_params(("parallel",)), name="inproj")(x2, g, w)


def _memkv_kernel(m_ref, g_ref, w_ref, kv_ref):
    h = _rms(m_ref[...], g_ref[...]).astype(BF16)
    kv_ref[...] = _dot(h, w_ref[...]).astype(BF16)


def _memkv(mem2, g, w, layer):
    n = mem2.shape[0]
    return pl.pallas_call(
        _memkv_kernel, grid=(n // ROW_TILE,),
        in_specs=[pl.BlockSpec((ROW_TILE, D_MODEL), lambda i: (i, 0)),
                  _const_spec((1, D_MODEL)), _layer_spec(w, layer)],
        out_specs=pl.BlockSpec((ROW_TILE, 2 * D_MODEL), lambda i: (i, 0)),
        out_shape=jax.ShapeDtypeStruct((n, 2 * D_MODEL), BF16),
        compiler_params=_params(("parallel",)), name="memkv")(mem2, g, w)


def _cross_kernel(x_ref, oa_ref, ob_ref, oc_ref, od_ref, wout_ref, g_ref, wq_ref, kv_ref, wo_ref, o_ref):
    x = x_ref[...]
    for j, ref in enumerate((oa_ref, ob_ref, oc_ref, od_ref)):
        x = x + _dot(ref[...], wout_ref[j * GROUP_WIDTH:(j + 1) * GROUP_WIDTH, :])
    h = _rms(x, g_ref[...]).astype(BF16)
    q = _dot(h, wq_ref[...]).astype(BF16)
    heads = []
    for hd in range(HEADS):
        lo = hd * X_HD
        s = _dot_t(q[:, lo:lo + X_HD], kv_ref[:, lo:lo + X_HD]) * (X_HD ** -0.5)
        s = s - jnp.max(s, axis=-1, keepdims=True)
        p = jnp.exp(s)
        p = p / jnp.sum(p, axis=-1, keepdims=True)
        heads.append(_dot(p.astype(BF16), kv_ref[:, D_MODEL + lo:D_MODEL + lo + X_HD]).astype(BF16))
    o = jnp.concatenate(heads, axis=-1)
    o_ref[...] = x + _dot(o, wo_ref[...])


def _cross(x2, outs, w_out, g, wq, kv, wo, layer, seq, mem_len):
    n = x2.shape[0]
    tiles_per_seq = seq // ROW_TILE
    row = pl.BlockSpec((ROW_TILE, D_MODEL), lambda i: (i, 0))
    grp = pl.BlockSpec((ROW_TILE, GROUP_WIDTH), lambda i: (i, 0))
    return pl.pallas_call(
        _cross_kernel, grid=(n // ROW_TILE,),
        in_specs=[row, grp, grp, grp, grp, _layer_spec(w_out, layer),
                  _const_spec((1, D_MODEL)), _layer_spec(wq, layer),
                  pl.BlockSpec((mem_len, 2 * D_MODEL), lambda i: (i // tiles_per_seq, 0)),
                  _layer_spec(wo, layer)],
        out_specs=row, out_shape=jax.ShapeDtypeStruct(x2.shape, F32),
        compiler_params=_params(("parallel",)), name="cross")(x2, *outs, w_out, g, wq, kv, wo)


def _gla_constants(dk):
    c, h = GLA_CHUNK, HEADS
    w, wv = h * dk, h * HEAD_DV
    r = np.arange(c)
    blocks = [(r[:, None] >= r[None, :])]
    masks = [(r[:, None] == r[None, :])]
    m = 1
    while m < c:
        ref = (r // (2 * m)) * (2 * m) + m - 1
        upper = (r % (2 * m)) >= m
        rp = r[None, :]
        rng_up = (rp > ref[:, None]) & (rp <= r[:, None])
        rng_lo = (rp > r[:, None]) & (rp <= ref[:, None])
        blocks.append(np.where(upper[:, None], rng_up, rng_lo))
        same = (r[:, None] // (2 * m)) == (r[None, :] // (2 * m))
        masks.append(same & upper[:, None] & (~upper)[None, :])
        m *= 2
    nmat = np.concatenate(blocks, axis=0).astype(np.float32)
    lvl = np.stack([np.tile(mk, (1, h)) for mk in masks]).astype(np.float32)
    rows = np.arange(h * c)[:, None] // c
    hm_k = (rows == (np.arange(w)[None, :] // dk)).astype(np.float32)
    hm_v = (rows == (np.arange(wv)[None, :] // HEAD_DV)).astype(np.float32)
    bm_t = ((np.arange(wv)[:, None] // HEAD_DV) == (np.arange(w)[None, :] // dk)).astype(np.float32)
    ones_blk = ((np.arange(wv)[:, None] // HEAD_DV) == (np.arange(wv)[None, :] // HEAD_DV))
    return (jnp.asarray(nmat, BF16), jnp.asarray(lvl, F32), jnp.asarray(hm_k, BF16),
            jnp.asarray(hm_v, BF16), jnp.asarray(bm_t, F32), jnp.asarray(ones_blk.astype(np.float32), BF16))


def _gla_recurrence(seq, q_ref, k_ref, v_ref, g_ref, o_ref, st_ref,
                    nmat_ref, lvl_ref, hmk_ref, hmv_ref, bmt_ref):
    c, h = GLA_CHUNK, HEADS
    n_lvl = lvl_ref.shape[0]
    st_ref[...] = jnp.zeros_like(st_ref)

    def step(ci, carry):
        r0 = pl.multiple_of(ci * c, c)
        q = q_ref[pl.ds(r0, c), :]
        k = k_ref[pl.ds(r0, c), :]
        v = v_ref[pl.ds(r0, c), :]
        g = g_ref[pl.ds(r0, c), :]
        g_hi = g.astype(BF16)
        g_mid = (g - g_hi.astype(F32)).astype(BF16)
        cum = nmat_ref[0:c, :]
        b = _dot(cum, g_hi) + _dot(cum, g_mid)
        lvl_rows = nmat_ref[c:, :]
        sums = _dot(lvl_rows, g_hi)
        hmk = hmk_ref[...]
        acc = jnp.zeros((c, h * c), F32)
        for li in range(n_lvl):
            if li == 0:
                qt, kt = q, k
            else:
                e = jnp.exp(sums[(li - 1) * c:li * c])
                qt, kt = q * e, k * e
            kst = jnp.concatenate([kt.astype(BF16)] * h, axis=0) * hmk
            acc = acc + _dot_t(qt.astype(BF16), kst) * lvl_ref[li]
        vb = v.astype(BF16)
        vst = jnp.concatenate([vb] * h, axis=0) * hmv_ref[...]
        o = _dot(acc.astype(BF16), vst)
        st = st_ref[...]
        o = o + _dot_t((q * jnp.exp(b)).astype(BF16), st.astype(BF16))
        b_last = b[c - 1:c, :]
        kh = (k * jnp.exp(b_last - b)).astype(BF16)
        st_ref[...] = st * jnp.exp(b_last) + _tdot(vb, kh) * bmt_ref[...]
        o_ref[pl.ds(r0, c), :] = o
        return carry

    lax.fori_loop(0, seq // c, step, 0, unroll=GLA_UNROLL)


def _head_rms_gate(o, gain, gate, ones_blk):
    sq = o * o
    hi = sq.astype(BF16)
    lo = (sq - hi.astype(F32)).astype(BF16)
    ms = (_dot(hi, ones_blk) + _dot(lo, ones_blk)) * (1.0 / HEAD_DV)
    return o * lax.rsqrt(ms + EPS) * gain * (gate * jax.nn.sigmoid(gate))


def _hgrn_kernel(p_ref, lb_ref, gain_ref, nmat_ref, lvl_ref, hmk_ref, hmv_ref, bmt_ref, ones_ref,
                 o_ref, q_s, k_s, g_s, o_s, st_s):
    seq, gw = o_ref.shape
    z = p_ref[:, gw:2 * gw]
    lb = lb_ref[...]
    lbf = jnp.maximum(lb, HG_LB_FLOOR)
    e = jnp.exp(-jnp.abs(z))
    inv = 1.0 / (1.0 + e)
    sig = jnp.where(z >= 0, inv, e * inv)
    nsig = jnp.where(z >= 0, e * inv, inv)
    f = lbf + (1.0 - lb) * sig
    g_s[...] = jnp.log(f)
    k_s[...] = (1.0 - lb) * nsig - (lbf - lb)
    q_s[...] = p_ref[:, 0:gw]
    _gla_recurrence(seq, q_s, k_s, p_ref.at[:, 2 * gw:3 * gw], g_s, o_s, st_s,
                    nmat_ref, lvl_ref, hmk_ref, hmv_ref, bmt_ref)
    o_ref[...] = _head_rms_gate(o_s[...], gain_ref[...], p_ref[:, 3 * gw:4 * gw],
                                ones_ref[...]).astype(o_ref.dtype)


def _gla_kernel(p_ref, gw_ref, gb_ref, gain_ref, nmat_ref, lvl_ref, hmk_ref, hmv_ref, bmt_ref,
                ones_ref, o_ref, q_s, g_s, o_s, st_s):
    seq, gw = o_ref.shape
    wk = HEADS * GLA_DK
    lr = p_ref[:, 2 * wk + 2 * gw:2 * wk + 2 * gw + LANES]
    lr_hi = lr.astype(BF16)
    lr_lo = (lr - lr_hi.astype(F32)).astype(BF16)
    w = gw_ref[...]
    w_hi = w.astype(BF16)
    w_lo = (w - w_hi.astype(F32)).astype(BF16)
    y = _dot(lr_hi, w_hi) + _dot(lr_hi, w_lo) + _dot(lr_lo, w_hi) + gb_ref[...]
    g_s[...] = (jnp.minimum(y, 0.0) - jnp.log(1.0 + jnp.exp(-jnp.abs(y)))) * (1.0 / GLA_TAU)
    q_s[...] = p_ref[:, 0:wk] * (GLA_DK ** -0.5)
    _gla_recurrence(seq, q_s, p_ref.at[:, wk:2 * wk], p_ref.at[:, 2 * wk:2 * wk + gw], g_s, o_s, st_s,
                    nmat_ref, lvl_ref, hmk_ref, hmv_ref, bmt_ref)
    o_ref[...] = _head_rms_gate(o_s[...], gain_ref[...], p_ref[:, 2 * wk + gw:2 * wk + 2 * gw],
                                ones_ref[...]).astype(o_ref.dtype)


def _seq_spec(seq, width):
    return pl.BlockSpec((seq, width), lambda b: (b, 0))


def _hgrn(pa, lb, gain, batch, seq):
    consts = _gla_constants(HG_DK)
    w = HEADS * HG_DK
    return pl.pallas_call(
        _hgrn_kernel, grid=(batch,),
        in_specs=[_seq_spec(seq, PA_W), _const_spec((1, w)), _const_spec((1, GROUP_WIDTH))]
                 + [_const_spec(cst.shape) for cst in consts],
        out_specs=_seq_spec(seq, GROUP_WIDTH),
        out_shape=jax.ShapeDtypeStruct((batch * seq, GROUP_WIDTH), BF16),
        scratch_shapes=[pltpu.VMEM((seq, w), F32), pltpu.VMEM((seq, w), F32), pltpu.VMEM((seq, w), F32),
                        pltpu.VMEM((seq, GROUP_WIDTH), F32), pltpu.VMEM((GROUP_WIDTH, w), F32)],
        compiler_params=_params(("parallel",)), name="hgrn")(pa, lb, gain, *consts)


def _gla(pb, gate_w, gate_b, gain, batch, seq):
    consts = _gla_constants(GLA_DK)
    w = HEADS * GLA_DK
    return pl.pallas_call(
        _gla_kernel, grid=(batch,),
        in_specs=[_seq_spec(seq, PB_W), _const_spec((LANES, w)), _const_spec((1, w)),
                  _const_spec((1, GROUP_WIDTH))] + [_const_spec(cst.shape) for cst in consts],
        out_specs=_seq_spec(seq, GROUP_WIDTH),
        out_shape=jax.ShapeDtypeStruct((batch * seq, GROUP_WIDTH), BF16),
        scratch_shapes=[pltpu.VMEM((seq, w), F32), pltpu.VMEM((seq, w), F32),
                        pltpu.VMEM((seq, GROUP_WIDTH), F32), pltpu.VMEM((GROUP_WIDTH, w), F32)],
        compiler_params=_params(("parallel",)), name="gla")(pb, gate_w, gate_b, gain, *consts)


def _dilated_kernel(p_ref, cos_ref, sin_ref, o_ref, q_s, k_s, v_s, op_s, lse_s):
    seq, gw = o_ref.shape
    n_half = gw // LANES
    heads_per_half = LANES // DA_HD
    lane = lax.broadcasted_iota(jnp.int32, (1, LANES), 1)
    in_head = lane & (DA_HD - 1)
    first_half = in_head < (DA_ROT // 2)
    cos, sin = cos_ref[...], sin_ref[...]

    def rope(t):
        partner = jnp.where(first_half, pltpu.roll(t, LANES - DA_ROT // 2, 1),
                            pltpu.roll(t, DA_ROT // 2, 1))
        return t * cos + partner * sin

    for hf in range(n_half):
        lo = hf * LANES
        q_s[hf] = rope(p_ref[:, lo:lo + LANES]) * (DA_HD ** -0.5 * LOG2_E)
        k_s[hf] = rope(p_ref[:, gw + lo:gw + lo + LANES])
        v_s[hf] = p_ref[:, 2 * gw + lo:2 * gw + lo + LANES]

    nq = DA_STEPS
    qi = lax.broadcasted_iota(jnp.int32, (nq, 2 * nq), 0)
    kj = lax.broadcasted_iota(jnp.int32, (nq, 2 * nq), 1)
    cur_ok = (kj >= nq) & (kj - nq <= qi)
    head_of_lane = lane >> int(math.log2(DA_HD))

    for pi, (window, dil) in enumerate(DA_PATTERNS):
        n_blk = seq // (dil * nq)

        def block(i, carry, pi=pi, dil=dil, n_blk=n_blk):
            r = i // n_blk
            n = i % n_blk
            cur0 = r + n * (nq * dil)
            prev0 = r + jnp.maximum(n - 1, 0) * (nq * dil)

            def rows(start):
                if dil == 1:
                    return pl.ds(pl.multiple_of(start, nq), nq)
                return pl.ds(start, nq, stride=dil)

            first_row = qi + jnp.where(n > 0, 0, nq)
            valid = cur_ok | ((kj < nq) & (kj >= first_row))
            for hf in range(n_half):
                q = q_s[hf, rows(cur0), :]
                kwin = jnp.concatenate([k_s[hf, rows(prev0), :], k_s[hf, rows(cur0), :]], axis=0).astype(BF16)
                vwin = jnp.concatenate([v_s[hf, rows(prev0), :], v_s[hf, rows(cur0), :]], axis=0).astype(BF16)
                o_acc = jnp.zeros((nq, LANES), F32)
                lse_acc = jnp.zeros((nq, LANES), F32)
                for hd in range(heads_per_half):
                    hm = head_of_lane == hd
                    s = _dot_t(jnp.where(hm, q, 0.0).astype(BF16), kwin)
                    s = jnp.where(valid, s, MASK_VALUE)
                    m = jnp.max(s, axis=-1, keepdims=True)
                    p = jnp.exp2(s - m)
                    l = jnp.sum(p, axis=-1, keepdims=True)
                    oh = _dot(p.astype(BF16), vwin) / l
                    o_acc = jnp.where(hm, oh, o_acc)
                    lse_acc = jnp.where(hm, m * (1.0 / LOG2_E) + jnp.log(l), lse_acc)
                op_s[pi, hf, rows(cur0), :] = o_acc
                lse_s[pi, hf, rows(cur0), :] = lse_acc
            return carry

        lax.fori_loop(0, dil * n_blk, block, 0, unroll=DA_UNROLL)

    for hf in range(n_half):
        lses = [lse_s[pi, hf] for pi in range(len(DA_PATTERNS))]
        mx = functools.reduce(jnp.maximum, lses)
        ws = [jnp.exp(l - mx) for l in lses]
        num = sum(wt * op_s[pi, hf] for pi, wt in enumerate(ws))
        o_ref[:, hf * LANES:(hf + 1) * LANES] = (num / sum(ws)).astype(o_ref.dtype)


def _dilated(pc, cos_t, sin_t, batch, seq):
    gw = GROUP_WIDTH
    n_pat = len(DA_PATTERNS)
    n_half = gw // LANES
    return pl.pallas_call(
        _dilated_kernel, grid=(batch,),
        in_specs=[_seq_spec(seq, PC_W), _seq_spec(seq, LANES), _seq_spec(seq, LANES)],
        out_specs=_seq_spec(seq, gw),
        out_shape=jax.ShapeDtypeStruct((batch * seq, gw), BF16),
        scratch_shapes=[pltpu.VMEM((n_half, seq, LANES), F32)] * 3
                       + [pltpu.VMEM((n_pat, n_half, seq, LANES), F32)] * 2,
        compiler_params=_params(("parallel",)), name="dilated")(pc, cos_t, sin_t)


def _conv_kernel(p_ref, w_ref, b_ref, lg_ref, lb_ref, o_ref, u_s):
    seq, gw = o_ref.shape
    a = p_ref[:, 0:gw]
    gate = p_ref[:, gw:2 * gw]
    u_s[0:CONV_PAD, :] = jnp.zeros((CONV_PAD, gw), F32)
    u_s[CONV_PAD:CONV_PAD + seq, :] = a * jax.nn.sigmoid(gate)
    u_s[CONV_PAD + seq:CONV_PAD + seq + SUBLANES, :] = jnp.zeros((SUBLANES, gw), F32)
    w = w_ref[...]
    first = CONV_PAD - (CONV_K - 1)
    ext = CONV_TILE + SUBLANES

    def tile(ti, carry):
        t0 = pl.multiple_of(ti * CONV_TILE, CONV_TILE)
        y = jnp.zeros((CONV_TILE, gw), F32) + b_ref[...]
        for s in range(SUBLANES):
            z = None
            for a in range(s, first + CONV_K, SUBLANES):
                if a >= first:
                    term = u_s[pl.ds(pl.multiple_of(t0 + (a - s), SUBLANES), ext), :] * w[a - first:a - first + 1, :]
                    z = term if z is None else z + term
            y = y + z[s:s + CONV_TILE, :]
        mu = jnp.mean(y, axis=-1, keepdims=True)
        d = y - mu
        var = jnp.mean(d * d, axis=-1, keepdims=True)
        yn = d * lax.rsqrt(var + EPS) * lg_ref[...] + lb_ref[...]
        o_ref[pl.ds(t0, CONV_TILE), :] = (yn * jax.nn.sigmoid(yn)).astype(o_ref.dtype)
        return carry

    lax.fori_loop(0, seq // CONV_TILE, tile, 0, unroll=CONV_UNROLL)


def _conv(pd, w, b, ln_g, ln_b, batch, seq):
    gw = GROUP_WIDTH
    vec = _const_spec((1, gw))
    return pl.pallas_call(
        _conv_kernel, grid=(batch,),
        in_specs=[_seq_spec(seq, PD_W), _const_spec(w.shape), vec, vec, vec],
        out_specs=_seq_spec(seq, gw),
        out_shape=jax.ShapeDtypeStruct((batch * seq, gw), BF16),
        scratch_shapes=[pltpu.VMEM((CONV_PAD + seq + SUBLANES, gw), F32)],
        compiler_params=_params(("parallel",)), name="conv")(pd, w, b, ln_g, ln_b)


def _rope_tables(positions):
    half = DA_ROT // 2
    inv_freq = jnp.power(jnp.float32(ROPE_THETA), -jnp.arange(0, DA_ROT, 2, dtype=F32) / DA_ROT)
    ang = positions.astype(F32)[..., None] * inv_freq
    cos, sin = jnp.cos(ang), jnp.sin(ang)
    rest = DA_HD - DA_ROT
    ones = jnp.ones(cos.shape[:-1] + (rest,), F32)
    cos_h = jnp.concatenate([cos, cos, ones], axis=-1)
    sin_h = jnp.concatenate([-sin, sin, 0.0 * ones], axis=-1)
    b, s = positions.shape
    tile = lambda t: jnp.tile(t, (1, 1, LANES // DA_HD)).reshape(b * s, LANES)
    return tile(cos_h), tile(sin_h)


def _relayout_w_in(w_in):
    a_end = PA_W
    wk = HEADS * GLA_DK
    b_qkv = w_in[..., a_end:a_end + 2 * wk + GROUP_WIDTH]
    lr0 = a_end + 2 * wk + GROUP_WIDTH
    b_lr = w_in[..., lr0:lr0 + GLA_RANK]
    b_r = w_in[..., lr0 + GLA_RANK:lr0 + GLA_RANK + GROUP_WIDTH]
    c0 = lr0 + GLA_RANK + GROUP_WIDTH
    rest = w_in[..., c0:]
    pad = jnp.zeros(w_in.shape[:-1] + (LANES - GLA_RANK,), w_in.dtype)
    return jnp.concatenate([w_in[..., :a_end], b_qkv, b_r, b_lr, pad, rest], axis=-1).astype(BF16)


def kernel(x, mem, positions, hgrn_lb_logits, ffn1_norm, ffn1_w_up, ffn1_w_down, mix_norm, w_in, hgrn_out_norm, gla_gate_w, gla_gate_b, gla_out_norm, conv_w, conv_b, conv_ln_g, conv_ln_b, w_out, cross_norm, mem_norm, cross_wq, cross_wkv, cross_wo, ffn2_norm, ffn2_w_up, ffn2_w_down, final_norm):
    batch, seq, d = x.shape
    mem_len = mem.shape[1]
    depth = w_in.shape[0]
    assert d == D_MODEL and seq % (DA_PATTERNS[-1][1] * DA_STEPS) == 0
    assert (batch * seq) % FFN_TILE == 0 and seq % ROW_TILE == 0 and (batch * mem_len) % ROW_TILE == 0

    cos_t, sin_t = _rope_tables(positions)
    p_lb = jax.nn.softmax(hgrn_lb_logits.astype(F32), axis=0)
    lower_bounds = jnp.cumsum(p_lb, axis=0) - p_lb[0:1]

    row = lambda v: v.reshape(1, -1).astype(F32)
    x2 = x.reshape(batch * seq, d)
    mem2 = mem.reshape(batch * mem_len, d)
    up1, down1 = ffn1_w_up.astype(BF16), ffn1_w_down.astype(BF16)
    up2, down2 = ffn2_w_up.astype(BF16), ffn2_w_down.astype(BF16)
    w_in_b, w_out_b = _relayout_w_in(w_in), w_out.astype(BF16)
    wq_b, wkv_b, wo_b = cross_wq.astype(BF16), cross_wkv.astype(BF16), cross_wo.astype(BF16)
    for l in range(depth):
        x2 = _ffn(x2, row(ffn1_norm[l]), up1, down1, l)

        pa, pb, pc, pd = _inproj(x2, row(mix_norm[l]), w_in_b, l)
        gate_w = jnp.zeros((LANES, HEADS * GLA_DK), F32).at[:GLA_RANK].set(gla_gate_w[l])
        o_a = _hgrn(pa, row(lower_bounds[l]), row(hgrn_out_norm[l]), batch, seq)
        o_b = _gla(pb, gate_w, row(gla_gate_b[l]), row(gla_out_norm[l]), batch, seq)
        o_c = _dilated(pc, cos_t, sin_t, batch, seq)
        o_d = _conv(pd, conv_w[l], row(conv_b[l]), row(conv_ln_g[l]), row(conv_ln_b[l]), batch, seq)
        kv = _memkv(mem2, row(mem_norm[l]), wkv_b, l)
        x2 = _cross(x2, (o_a, o_b, o_c, o_d), w_out_b, row(cross_norm[l]), wq_b, kv, wo_b, l, seq, mem_len)

        last = l == depth - 1
        x2 = _ffn(x2, row(ffn2_norm[l]), up2, down2, l, final_g=row(final_norm) if last else None)
    return x2.reshape(batch, seq, d)
```

```python
import functools
import math

import numpy as np
import jax
import jax.numpy as jnp
from jax import lax
from jax.experimental import pallas as pl
from jax.experimental.pallas import tpu as pltpu

F32 = jnp.float32
BF16 = jnp.bfloat16

D_MODEL = 1024
GROUP_WIDTH = D_MODEL // 4
HEADS = 4
HG_DK = GROUP_WIDTH // HEADS
GLA_DK = GROUP_WIDTH // (2 * HEADS)
HEAD_DV = GROUP_WIDTH // HEADS
GLA_RANK = 16
GLA_TAU = 16.0
HG_LB_FLOOR = 1e-20
DA_HD = GROUP_WIDTH // HEADS
DA_ROT = DA_HD // 4
ROPE_THETA = 500000.0
DA_PATTERNS = ((128, 1), (512, 4), (2048, 16))
MASK_VALUE = -1e30
LOG2_E = math.log2(math.e)
CONV_K = 31
X_HD = D_MODEL // HEADS
D_FF = ((int(8 * D_MODEL / 3) + 255) // 256) * 256
EPS = 1e-6

LANES = 128
SUBLANES = 8
ROW_TILE = 1024
FFN_TILE = 1024
MXU_DIM = 256
FF_CHUNK = 4 * MXU_DIM
GLA_CHUNK = 64
GLA_UNROLL = 16
DA_STEPS = 128
DA_UNROLL = 8
CONV_TILE = 128
CONV_UNROLL = 4
CONV_PAD = 32
VMEM_LIMIT = 56 * 1024 * 1024

PA_W = 4 * GROUP_WIDTH
PB_W = 2 * HEADS * GLA_DK + 2 * GROUP_WIDTH + LANES
PC_W = 3 * GROUP_WIDTH
PD_W = 2 * GROUP_WIDTH


def _dot(a, b):
    return jnp.dot(a, b, preferred_element_type=F32)


def _dot_t(a, b):
    return lax.dot_general(a, b, (((1,), (1,)), ((), ())), preferred_element_type=F32)


def _tdot(a, b):
    return lax.dot_general(a, b, (((0,), (0,)), ((), ())), preferred_element_type=F32)


def _rms(x, g):
    ms = jnp.mean(x * x, axis=-1, keepdims=True)
    return x * lax.rsqrt(ms + EPS) * g


def _split3(x):
    hi = x.astype(BF16)
    r1 = x - hi.astype(F32)
    mid = r1.astype(BF16)
    lo = (r1 - mid.astype(F32)).astype(BF16)
    return hi, mid, lo


def _const_spec(shape):
    nd = len(shape)
    return pl.BlockSpec(shape, lambda *_: (0,) * nd, pipeline_mode=pl.Buffered(1))


def _layer_spec(stacked, layer):
    nd = stacked.ndim - 1
    return pl.BlockSpec((None,) + stacked.shape[1:], lambda *_: (layer,) + (0,) * nd,
                        pipeline_mode=pl.Buffered(1))


def _params(sem):
    return pltpu.CompilerParams(dimension_semantics=sem, vmem_limit_bytes=VMEM_LIMIT)


def _ffn_body(x_ref, g_ref, wup_ref, wd_ref):
    x = x_ref[...]
    h = _rms(x, g_ref[...]).astype(BF16)
    acc = jnp.zeros_like(x)
    for lo in range(0, D_FF, FF_CHUNK):
        width = min(FF_CHUNK, D_FF - lo)
        gate = _dot(h, wup_ref[:, lo:lo + width])
        up = _dot(h, wup_ref[:, D_FF + lo:D_FF + lo + width])
        a = (gate * jax.nn.sigmoid(gate) * up).astype(BF16)
        acc = acc + _dot(a, wd_ref[lo:lo + width, :])
    return x + 0.5 * acc


def _ffn_kernel(x_ref, g_ref, wup_ref, wd_ref, o_ref):
    o_ref[...] = _ffn_body(x_ref, g_ref, wup_ref, wd_ref)


def _ffn_final_kernel(x_ref, g_ref, wup_ref, wd_ref, fg_ref, o_ref):
    o_ref[...] = _rms(_ffn_body(x_ref, g_ref, wup_ref, wd_ref), fg_ref[...])


def _ffn(x2, g, wup, wd, layer, final_g=None):
    n = x2.shape[0]
    row = pl.BlockSpec((FFN_TILE, D_MODEL), lambda i: (i, 0))
    in_specs = [row, _const_spec((1, D_MODEL)), _layer_spec(wup, layer), _layer_spec(wd, layer)]
    args = [x2, g, wup, wd]
    body = _ffn_kernel
    if final_g is not None:
        in_specs.append(_const_spec((1, D_MODEL)))
        args.append(final_g)
        body = _ffn_final_kernel
    return pl.pallas_call(
        body, grid=(n // FFN_TILE,), in_specs=in_specs, out_specs=row,
        out_shape=jax.ShapeDtypeStruct(x2.shape, F32),
        compiler_params=_params(("parallel",)), name="ffn")(*args)


def _inproj_kernel(x_ref, g_ref, w_ref, pa_ref, pb_ref, pc_ref, pd_ref):
    h = _rms(x_ref[...], g_ref[...]).astype(BF16)
    off = 0
    for ref, width in ((pa_ref, PA_W), (pb_ref, PB_W), (pc_ref, PC_W), (pd_ref, PD_W)):
        ref[...] = _dot(h, w_ref[:, off:off + width])
        off += width


def _inproj(x2, g, w, layer):
    n = x2.shape[0]
    widths = (PA_W, PB_W, PC_W, PD_W)
    return pl.pallas_call(
        _inproj_kernel, grid=(n // ROW_TILE,),
        in_specs=[pl.BlockSpec((ROW_TILE, D_MODEL), lambda i: (i, 0)),
                  _const_spec((1, D_MODEL)), _layer_spec(w, layer)],
        out_specs=[pl.BlockSpec((ROW_TILE, wd), lambda i: (i, 0)) for wd in widths],
        out_shape=[jax.ShapeDtypeStruct((n, wd), F32) for wd in widths],
        compiler_params=_params(("parallel",)), name="inproj")(x2, g, w)


def _memkv_kernel(m_ref, g_ref, w_ref, kv_ref):
    h = _rms(m_ref[...], g_ref[...]).astype(BF16)
    kv_ref[...] = _dot(h, w_ref[...]).astype(BF16)


def _memkv(mem2, g, w, layer):
    n = mem2.shape[0]
    return pl.pallas_call(
        _memkv_kernel, grid=(n // ROW_TILE,),
        in_specs=[pl.BlockSpec((ROW_TILE, D_MODEL), lambda i: (i, 0)),
                  _const_spec((1, D_MODEL)), _layer_spec(w, layer)],
        out_specs=pl.BlockSpec((ROW_TILE, 2 * D_MODEL), lambda i: (i, 0)),
        out_shape=jax.ShapeDtypeStruct((n, 2 * D_MODEL), BF16),
        compiler_params=_params(("parallel",)), name="memkv")(mem2, g, w)


def _cross_kernel(x_ref, oa_ref, ob_ref, oc_ref, od_ref, wout_ref, g_ref, wq_ref, kv_ref, wo_ref, o_ref):
    x = x_ref[...]
    for j, ref in enumerate((oa_ref, ob_ref, oc_ref, od_ref)):
        x = x + _dot(ref[...], wout_ref[j * GROUP_WIDTH:(j + 1) * GROUP_WIDTH, :])
    h = _rms(x, g_ref[...]).astype(BF16)
    q = _dot(h, wq_ref[...]).astype(BF16)
    heads = []
    for hd in range(HEADS):
        lo = hd * X_HD
        s = _dot_t(q[:, lo:lo + X_HD], kv_ref[:, lo:lo + X_HD]) * (X_HD ** -0.5)
        s = s - jnp.max(s, axis=-1, keepdims=True)
        p = jnp.exp(s)
        p = p / jnp.sum(p, axis=-1, keepdims=True)
        heads.append(_dot(p.astype(BF16), kv_ref[:, D_MODEL + lo:D_MODEL + lo + X_HD]).astype(BF16))
    o = jnp.concatenate(heads, axis=-1)
    o_ref[...] = x + _dot(o, wo_ref[...])


def _cross(x2, outs, w_out, g, wq, kv, wo, layer, seq, mem_len):
    n = x2.shape[0]
    tiles_per_seq = seq // ROW_TILE
    row = pl.BlockSpec((ROW_TILE, D_MODEL), lambda i: (i, 0))
    grp = pl.BlockSpec((ROW_TILE, GROUP_WIDTH), lambda i: (i, 0))
    return pl.pallas_call(
        _cross_kernel, grid=(n // ROW_TILE,),
        in_specs=[row, grp, grp, grp, grp, _layer_spec(w_out, layer),
                  _const_spec((1, D_MODEL)), _layer_spec(wq, layer),
                  pl.BlockSpec((mem_len, 2 * D_MODEL), lambda i: (i // tiles_per_seq, 0)),
                  _layer_spec(wo, layer)],
        out_specs=row, out_shape=jax.ShapeDtypeStruct(x2.shape, F32),
        compiler_params=_params(("parallel",)), name="cross")(x2, *outs, w_out, g, wq, kv, wo)


def _gla_constants(dk):
    c, h = GLA_CHUNK, HEADS
    w, wv = h * dk, h * HEAD_DV
    r = np.arange(c)
    blocks = [(r[:, None] >= r[None, :])]
    masks = [(r[:, None] == r[None, :])]
    m = 1
    while m < c:
        ref = (r // (2 * m)) * (2 * m) + m - 1
        upper = (r % (2 * m)) >= m
        rp = r[None, :]
        rng_up = (rp > ref[:, None]) & (rp <= r[:, None])
        rng_lo = (rp > r[:, None]) & (rp <= ref[:, None])
        blocks.append(np.where(upper[:, None], rng_up, rng_lo))
        same = (r[:, None] // (2 * m)) == (r[None, :] // (2 * m))
        masks.append(same & upper[:, None] & (~upper)[None, :])
        m *= 2
    nmat = np.concatenate(blocks, axis=0).astype(np.float32)
    lvl = np.stack([np.tile(mk, (1, h)) for mk in masks]).astype(np.float32)
    rows = np.arange(h * c)[:, None] // c
    hm_k = (rows == (np.arange(w)[None, :] // dk)).astype(np.float32)
    hm_v = (rows == (np.arange(wv)[None, :] // HEAD_DV)).astype(np.float32)
    bm_t = ((np.arange(wv)[:, None] // HEAD_DV) == (np.arange(w)[None, :] // dk)).astype(np.float32)
    ones_blk = ((np.arange(wv)[:, None] // HEAD_DV) == (np.arange(wv)[None, :] // HEAD_DV))
    return (jnp.asarray(nmat, BF16), jnp.asarray(lvl, F32), jnp.asarray(hm_k, BF16),
            jnp.asarray(hm_v, BF16), jnp.asarray(bm_t, F32), jnp.asarray(ones_blk.astype(np.float32), BF16))


def _gla_recurrence(seq, q_ref, k_ref, v_ref, g_ref, o_ref, st_ref,
                    nmat_ref, lvl_ref, hmk_ref, hmv_ref, bmt_ref):
    c, h = GLA_CHUNK, HEADS
    n_lvl = lvl_ref.shape[0]
    st_ref[...] = jnp.zeros_like(st_ref)

    def step(ci, carry):
        r0 = pl.multiple_of(ci * c, c)
        q = q_ref[pl.ds(r0, c), :]
        k = k_ref[pl.ds(r0, c), :]
        v = v_ref[pl.ds(r0, c), :]
        g = g_ref[pl.ds(r0, c), :]
        g_hi = g.astype(BF16)
        g_mid = (g - g_hi.astype(F32)).astype(BF16)
        cum = nmat_ref[0:c, :]
        b = _dot(cum, g_hi) + _dot(cum, g_mid)
        lvl_rows = nmat_ref[c:, :]
        sums = _dot(lvl_rows, g_hi)
        hmk = hmk_ref[...]
        acc = jnp.zeros((c, h * c), F32)
        for li in range(n_lvl):
            if li == 0:
                qt, kt = q, k
            else:
                e = jnp.exp(sums[(li - 1) * c:li * c])
                qt, kt = q * e, k * e
            kst = jnp.concatenate([kt.astype(BF16)] * h, axis=0) * hmk
            acc = acc + _dot_t(qt.astype(BF16), kst) * lvl_ref[li]
        vb = v.astype(BF16)
        vst = jnp.concatenate([vb] * h, axis=0) * hmv_ref[...]
        o = _dot(acc.astype(BF16), vst)
        st = st_ref[...]
        o = o + _dot_t((q * jnp.exp(b)).astype(BF16), st.astype(BF16))
        b_last = b[c - 1:c, :]
        kh = (k * jnp.exp(b_last - b)).astype(BF16)
        st_ref[...] = st * jnp.exp(b_last) + _tdot(vb, kh) * bmt_ref[...]
        o_ref[pl.ds(r0, c), :] = o
        return carry

    lax.fori_loop(0, seq // c, step, 0, unroll=GLA_UNROLL)


def _head_rms_gate(o, gain, gate, ones_blk):
    sq = o * o
    hi = sq.astype(BF16)
    lo = (sq - hi.astype(F32)).astype(BF16)
    ms = (_dot(hi, ones_blk) + _dot(lo, ones_blk)) * (1.0 / HEAD_DV)
    return o * lax.rsqrt(ms + EPS) * gain * (gate * jax.nn.sigmoid(gate))


def _hgrn_kernel(p_ref, lb_ref, gain_ref, nmat_ref, lvl_ref, hmk_ref, hmv_ref, bmt_ref, ones_ref,
                 o_ref, q_s, k_s, g_s, o_s, st_s):
    seq, gw = o_ref.shape
    z = p_ref[:, gw:2 * gw]
    lb = lb_ref[...]
    lbf = jnp.maximum(lb, HG_LB_FLOOR)
    e = jnp.exp(-jnp.abs(z))
    inv = 1.0 / (1.0 + e)
    sig = jnp.where(z >= 0, inv, e * inv)
    nsig = jnp.where(z >= 0, e * inv, inv)
    f = lbf + (1.0 - lb) * sig
    g_s[...] = jnp.log(f)
    k_s[...] = (1.0 - lb) * nsig - (lbf - lb)
    q_s[...] = p_ref[:, 0:gw]
    _gla_recurrence(seq, q_s, k_s, p_ref.at[:, 2 * gw:3 * gw], g_s, o_s, st_s,
                    nmat_ref, lvl_ref, hmk_ref, hmv_ref, bmt_ref)
    o_ref[...] = _head_rms_gate(o_s[...], gain_ref[...], p_ref[:, 3 * gw:4 * gw],
                                ones_ref[...]).astype(o_ref.dtype)


def _gla_kernel(p_ref, gw_ref, gb_ref, gain_ref, nmat_ref, lvl_ref, hmk_ref, hmv_ref, bmt_ref,
                ones_ref, o_ref, q_s, g_s, o_s, st_s):
    seq, gw = o_ref.shape
    wk = HEADS * GLA_DK
    lr = p_ref[:, 2 * wk + 2 * gw:2 * wk + 2 * gw + LANES]
    lr_hi = lr.astype(BF16)
    lr_lo = (lr - lr_hi.astype(F32)).astype(BF16)
    w = gw_ref[...]
    w_hi = w.astype(BF16)
    w_lo = (w - w_hi.astype(F32)).astype(BF16)
    y = _dot(lr_hi, w_hi) + _dot(lr_hi, w_lo) + _dot(lr_lo, w_hi) + gb_ref[...]
    g_s[...] = (jnp.minimum(y, 0.0) - jnp.log(1.0 + jnp.exp(-jnp.abs(y)))) * (1.0 / GLA_TAU)
    q_s[...] = p_ref[:, 0:wk] * (GLA_DK ** -0.5)
    _gla_recurrence(seq, q_s, p_ref.at[:, wk:2 * wk], p_ref.at[:, 2 * wk:2 * wk + gw], g_s, o_s, st_s,
                    nmat_ref, lvl_ref, hmk_ref, hmv_ref, bmt_ref)
    o_ref[...] = _head_rms_gate(o_s[...], gain_ref[...], p_ref[:, 2 * wk + gw:2 * wk + 2 * gw],
                                ones_ref[...]).astype(o_ref.dtype)


def _seq_spec(seq, width):
    return pl.BlockSpec((seq, width), lambda b: (b, 0))


def _hgrn(pa, lb, gain, batch, seq):
    consts = _gla_constants(HG_DK)
    w = HEADS * HG_DK
    return pl.pallas_call(
        _hgrn_kernel, grid=(batch,),
        in_specs=[_seq_spec(seq, PA_W), _const_spec((1, w)), _const_spec((1, GROUP_WIDTH))]
                 + [_const_spec(cst.shape) for cst in consts],
        out_specs=_seq_spec(seq, GROUP_WIDTH),
        out_shape=jax.ShapeDtypeStruct((batch * seq, GROUP_WIDTH), BF16),
        scratch_shapes=[pltpu.VMEM((seq, w), F32), pltpu.VMEM((seq, w), F32), pltpu.VMEM((seq, w), F32),
                        pltpu.VMEM((seq, GROUP_WIDTH), F32), pltpu.VMEM((GROUP_WIDTH, w), F32)],
        compiler_params=_params(("parallel",)), name="hgrn")(pa, lb, gain, *consts)


def _gla(pb, gate_w, gate_b, gain, batch, seq):
    consts = _gla_constants(GLA_DK)
    w = HEADS * GLA_DK
    return pl.pallas_call(
        _gla_kernel, grid=(batch,),
        in_specs=[_seq_spec(seq, PB_W), _const_spec((LANES, w)), _const_spec((1, w)),
                  _const_spec((1, GROUP_WIDTH))] + [_const_spec(cst.shape) for cst in consts],
        out_specs=_seq_spec(seq, GROUP_WIDTH),
        out_shape=jax.ShapeDtypeStruct((batch * seq, GROUP_WIDTH), BF16),
        scratch_shapes=[pltpu.VMEM((seq, w), F32), pltpu.VMEM((seq, w), F32),
                        pltpu.VMEM((seq, GROUP_WIDTH), F32), pltpu.VMEM((GROUP_WIDTH, w), F32)],
        compiler_params=_params(("parallel",)), name="gla")(pb, gate_w, gate_b, gain, *consts)


def _dilated_kernel(p_ref, cos_ref, sin_ref, o_ref, q_s, k_s, v_s, op_s, lse_s):
    seq, gw = o_ref.shape
    n_half = gw // LANES
    heads_per_half = LANES // DA_HD
    lane = lax.broadcasted_iota(jnp.int32, (1, LANES), 1)
    cos, sin = cos_ref[...], sin_ref[...]

    src = lax.broadcasted_iota(jnp.int32, (LANES, LANES), 0)
    dst = lax.broadcasted_iota(jnp.int32, (LANES, LANES), 1)
    dst_in_head = dst & (DA_HD - 1)
    half = DA_ROT // 2
    perm = (((dst_in_head < half) & (src == dst + half))
            | ((dst_in_head >= half) & (dst_in_head < DA_ROT) & (src == dst - half)))
    perm = jnp.where(perm, 1.0, 0.0).astype(BF16)

    def rope(t):
        t_hi = t.astype(BF16)
        t_lo = (t - t_hi.astype(F32)).astype(BF16)
        partner = _dot(t_hi, perm) + _dot(t_lo, perm)
        return t * cos + partner * sin

    for hf in range(n_half):
        lo = hf * LANES
        q_s[hf] = rope(p_ref[:, lo:lo + LANES]) * (DA_HD ** -0.5 * LOG2_E)
        k_s[hf] = rope(p_ref[:, gw + lo:gw + lo + LANES])
        v_s[hf] = p_ref[:, 2 * gw + lo:2 * gw + lo + LANES]

    nq = DA_STEPS
    qi = lax.broadcasted_iota(jnp.int32, (nq, 2 * nq), 0)
    kj = lax.broadcasted_iota(jnp.int32, (nq, 2 * nq), 1)
    cur_ok = (kj >= nq) & (kj - nq <= qi)
    head_of_lane = lane >> int(math.log2(DA_HD))

    for pi, (window, dil) in enumerate(DA_PATTERNS):
        n_blk = seq // (dil * nq)

        def block(i, carry, pi=pi, dil=dil, n_blk=n_blk):
            r = i // n_blk
            n = i % n_blk
            cur0 = r + n * (nq * dil)
            prev0 = r + jnp.maximum(n - 1, 0) * (nq * dil)

            def rows(start):
                if dil == 1:
                    return pl.ds(pl.multiple_of(start, nq), nq)
                return pl.ds(start, nq, stride=dil)

            first_row = qi + jnp.where(n > 0, 0, nq)
            valid = cur_ok | ((kj < nq) & (kj >= first_row))
            for hf in range(n_half):
                q = q_s[hf, rows(cur0), :]
                kwin = jnp.concatenate([k_s[hf, rows(prev0), :], k_s[hf, rows(cur0), :]], axis=0).astype(BF16)
                vwin = jnp.concatenate([v_s[hf, rows(prev0), :], v_s[hf, rows(cur0), :]], axis=0).astype(BF16)
                o_acc = jnp.zeros((nq, LANES), F32)
                lse_acc = jnp.zeros((nq, LANES), F32)
                for hd in range(heads_per_half):
                    hm = head_of_lane == hd
                    s = _dot_t(jnp.where(hm, q, 0.0).astype(BF16), kwin)
                    s = jnp.where(valid, s, MASK_VALUE)
                    m = jnp.max(s, axis=-1, keepdims=True)
                    p = jnp.exp2(s - m)
                    l = jnp.sum(p, axis=-1, keepdims=True)
                    oh = _dot(p.astype(BF16), vwin) / l
                    o_acc = jnp.where(hm, oh, o_acc)
                    lse_acc = jnp.where(hm, m * (1.0 / LOG2_E) + jnp.log(l), lse_acc)
                op_s[pi, hf, rows(cur0), :] = o_acc
                lse_s[pi, hf, rows(cur0), :] = lse_acc
            return carry

        lax.fori_loop(0, dil * n_blk, block, 0, unroll=DA_UNROLL)

    for hf in range(n_half):
        lses = [lse_s[pi, hf] for pi in range(len(DA_PATTERNS))]
        mx = functools.reduce(jnp.maximum, lses)
        ws = [jnp.exp(l - mx) for l in lses]
        num = sum(wt * op_s[pi, hf] for pi, wt in enumerate(ws))
        o_ref[:, hf * LANES:(hf + 1) * LANES] = (num / sum(ws)).astype(o_ref.dtype)


def _dilated(pc, cos_t, sin_t, batch, seq):
    gw = GROUP_WIDTH
    n_pat = len(DA_PATTERNS)
    n_half = gw // LANES
    return pl.pallas_call(
        _dilated_kernel, grid=(batch,),
        in_specs=[_seq_spec(seq, PC_W), _seq_spec(seq, LANES), _seq_spec(seq, LANES)],
        out_specs=_seq_spec(seq, gw),
        out_shape=jax.ShapeDtypeStruct((batch * seq, gw), BF16),
        scratch_shapes=[pltpu.VMEM((n_half, seq, LANES), F32)] * 3
                       + [pltpu.VMEM((n_pat, n_half, seq, LANES), F32)] * 2,
        compiler_params=_params(("parallel",)), name="dilated")(pc, cos_t, sin_t)


def _conv_kernel(p_ref, w_ref, b_ref, lg_ref, lb_ref, o_ref, u_s):
    seq, gw = o_ref.shape
    a = p_ref[:, 0:gw]
    gate = p_ref[:, gw:2 * gw]
    u_s[0:CONV_PAD, :] = jnp.zeros((CONV_PAD, gw), F32)
    u_s[CONV_PAD:CONV_PAD + seq, :] = a * jax.nn.sigmoid(gate)
    u_s[CONV_PAD + seq:CONV_PAD + seq + SUBLANES, :] = jnp.zeros((SUBLANES, gw), F32)
    w = w_ref[...]
    first = CONV_PAD - (CONV_K - 1)
    ext = CONV_TILE + SUBLANES

    def tile(ti, carry):
        t0 = pl.multiple_of(ti * CONV_TILE, CONV_TILE)
        y = jnp.zeros((CONV_TILE, gw), F32) + b_ref[...]
        for s in range(SUBLANES):
            z = None
            for a in range(s, first + CONV_K, SUBLANES):
                if a >= first:
                    term = u_s[pl.ds(pl.multiple_of(t0 + (a - s), SUBLANES), ext), :] * w[a - first:a - first + 1, :]
                    z = term if z is None else z + term
            y = y + z[s:s + CONV_TILE, :]
        mu = jnp.mean(y, axis=-1, keepdims=True)
        d = y - mu
        var = jnp.mean(d * d, axis=-1, keepdims=True)
        yn = d * lax.rsqrt(var + EPS) * lg_ref[...] + lb_ref[...]
        o_ref[pl.ds(t0, CONV_TILE), :] = (yn * jax.nn.sigmoid(yn)).astype(o_ref.dtype)
        return carry

    lax.fori_loop(0, seq // CONV_TILE, tile, 0, unroll=CONV_UNROLL)


def _conv(pd, w, b, ln_g, ln_b, batch, seq):
    gw = GROUP_WIDTH
    vec = _const_spec((1, gw))
    return pl.pallas_call(
        _conv_kernel, grid=(batch,),
        in_specs=[_seq_spec(seq, PD_W), _const_spec(w.shape), vec, vec, vec],
        out_specs=_seq_spec(seq, gw),
        out_shape=jax.ShapeDtypeStruct((batch * seq, gw), BF16),
        scratch_shapes=[pltpu.VMEM((CONV_PAD + seq + SUBLANES, gw), F32)],
        compiler_params=_params(("parallel",)), name="conv")(pd, w, b, ln_g, ln_b)


def _rope_tables(positions):
    half = DA_ROT // 2
    inv_freq = jnp.power(jnp.float32(ROPE_THETA), -jnp.arange(0, DA_ROT, 2, dtype=F32) / DA_ROT)
    ang = positions.astype(F32)[..., None] * inv_freq
    cos, sin = jnp.cos(ang), jnp.sin(ang)
    rest = DA_HD - DA_ROT
    ones = jnp.ones(cos.shape[:-1] + (rest,), F32)
    cos_h = jnp.concatenate([cos, cos, ones], axis=-1)
    sin_h = jnp.concatenate([-sin, sin, 0.0 * ones], axis=-1)
    b, s = positions.shape
    tile = lambda t: jnp.tile(t, (1, 1, LANES // DA_HD)).reshape(b * s, LANES)
    return tile(cos_h), tile(sin_h)


def _relayout_w_in(w_in):
    a_end = PA_W
    wk = HEADS * GLA_DK
    b_qkv = w_in[..., a_end:a_end + 2 * wk + GROUP_WIDTH]
    lr0 = a_end + 2 * wk + GROUP_WIDTH
    b_lr = w_in[..., lr0:lr0 + GLA_RANK]
    b_r = w_in[..., lr0 + GLA_RANK:lr0 + GLA_RANK + GROUP_WIDTH]
    c0 = lr0 + GLA_RANK + GROUP_WIDTH
    rest = w_in[..., c0:]
    pad = jnp.zeros(w_in.shape[:-1] + (LANES - GLA_RANK,), w_in.dtype)
    return jnp.concatenate([w_in[..., :a_end], b_qkv, b_r, b_lr, pad, rest], axis=-1).astype(BF16)


def kernel(x, mem, positions, hgrn_lb_logits, ffn1_norm, ffn1_w_up, ffn1_w_down, mix_norm, w_in, hgrn_out_norm, gla_gate_w, gla_gate_b, gla_out_norm, conv_w, conv_b, conv_ln_g, conv_ln_b, w_out, cross_norm, mem_norm, cross_wq, cross_wkv, cross_wo, ffn2_norm, ffn2_w_up, ffn2_w_down, final_norm):
    batch, seq, d = x.shape
    mem_len = mem.shape[1]
    depth = w_in.shape[0]
    assert d == D_MODEL and seq % (DA_PATTERNS[-1][1] * DA_STEPS) == 0
    assert (batch * seq) % FFN_TILE == 0 and seq % ROW_TILE == 0 and (batch * mem_len) % ROW_TILE == 0

    cos_t, sin_t = _rope_tables(positions)
    p_lb = jax.nn.softmax(hgrn_lb_logits.astype(F32), axis=0)
    lower_bounds = jnp.cumsum(p_lb, axis=0) - p_lb[0:1]

    row = lambda v: v.reshape(1, -1).astype(F32)
    x2 = x.reshape(batch * seq, d)
    mem2 = mem.reshape(batch * mem_len, d)
    up1, down1 = ffn1_w_up.astype(BF16), ffn1_w_down.astype(BF16)
    up2, down2 = ffn2_w_up.astype(BF16), ffn2_w_down.astype(BF16)
    w_in_b, w_out_b = _relayout_w_in(w_in), w_out.astype(BF16)
    wq_b, wkv_b, wo_b = cross_wq.astype(BF16), cross_wkv.astype(BF16), cross_wo.astype(BF16)
    for l in range(depth):
        x2 = _ffn(x2, row(ffn1_norm[l]), up1, down1, l)

        pa, pb, pc, pd = _inproj(x2, row(mix_norm[l]), w_in_b, l)
        gate_w = jnp.zeros((LANES, HEADS * GLA_DK), F32).at[:GLA_RANK].set(gla_gate_w[l])
        o_a = _hgrn(pa, row(lower_bounds[l]), row(hgrn_out_norm[l]), batch, seq)
        o_b = _gla(pb, gate_w, row(gla_gate_b[l]), row(gla_out_norm[l]), batch, seq)
        o_c = _dilated(pc, cos_t, sin_t, batch, seq)
        o_d = _conv(pd, conv_w[l], row(conv_b[l]), row(conv_ln_g[l]), row(conv_ln_b[l]), batch, seq)
        kv = _memkv(mem2, row(mem_norm[l]), wkv_b, l)
        x2 = _cross(x2, (o_a, o_b, o_c, o_d), w_out_b, row(cross_norm[l]), wq_b, kv, wo_b, l, seq, mem_len)

        last = l == depth - 1
        x2 = _ffn(x2, row(ffn2_norm[l]), up2, down2, l, final_g=row(final_norm) if last else None)
    return x2.reshape(batch, seq, d)
```

```python
import functools
import math

import numpy as np
import jax
import jax.numpy as jnp
from jax import lax
from jax.experimental import pallas as pl
from jax.experimental.pallas import tpu as pltpu

F32 = jnp.float32
BF16 = jnp.bfloat16

D_MODEL = 1024
GROUP_WIDTH = D_MODEL // 4
HEADS = 4
HG_DK = GROUP_WIDTH // HEADS
GLA_DK = GROUP_WIDTH // (2 * HEADS)
HEAD_DV = GROUP_WIDTH // HEADS
GLA_RANK = 16
GLA_TAU = 16.0
HG_LB_FLOOR = 1e-20
DA_HD = GROUP_WIDTH // HEADS
DA_ROT = DA_HD // 4
ROPE_THETA = 500000.0
DA_PATTERNS = ((128, 1), (512, 4), (2048, 16))
MASK_VALUE = -1e30
LOG2_E = math.log2(math.e)
CONV_K = 31
X_HD = D_MODEL // HEADS
D_FF = ((int(8 * D_MODEL / 3) + 255) // 256) * 256
EPS = 1e-6

LANES = 128
SUBLANES = 8
ROW_TILE = 1024
MXU_DIM = 256
FF_CHUNK = 4 * MXU_DIM
GLA_CHUNK = 64
GLA_UNROLL = 16
DA_STEPS = 128
DA_UNROLL = 8
CONV_TILE = 128
CONV_UNROLL = 8
CONV_PAD = 32
VMEM_LIMIT = 56 * 1024 * 1024

PA_W = 4 * GROUP_WIDTH
PB_W = 2 * HEADS * GLA_DK + 2 * GROUP_WIDTH + LANES
PC_W = 3 * GROUP_WIDTH
PD_W = 2 * GROUP_WIDTH


def _dot(a, b):
    return jnp.dot(a, b, preferred_element_type=F32)


def _dot_t(a, b):
    return lax.dot_general(a, b, (((1,), (1,)), ((), ())), preferred_element_type=F32)


def _tdot(a, b):
    return lax.dot_general(a, b, (((0,), (0,)), ((), ())), preferred_element_type=F32)


def _rms(x, g):
    ms = jnp.mean(x * x, axis=-1, keepdims=True)
    return x * lax.rsqrt(ms + EPS) * g


def _const_spec(shape):
    nd = len(shape)
    return pl.BlockSpec(shape, lambda *_: (0,) * nd, pipeline_mode=pl.Buffered(1))


def _layer_spec(stacked, layer):
    nd = stacked.ndim - 1
    return pl.BlockSpec((None,) + stacked.shape[1:], lambda *_: (layer,) + (0,) * nd,
                        pipeline_mode=pl.Buffered(1))


def _params(sem):
    return pltpu.CompilerParams(dimension_semantics=sem, vmem_limit_bytes=VMEM_LIMIT)


def _ffn_body(x_ref, g_ref, wup_ref, wd_ref):
    x = x_ref[...]
    h = _rms(x, g_ref[...]).astype(BF16)
    acc = jnp.zeros_like(x)
    for lo in range(0, D_FF, FF_CHUNK):
        width = min(FF_CHUNK, D_FF - lo)
        gate = _dot(h, wup_ref[:, lo:lo + width])
        up = _dot(h, wup_ref[:, D_FF + lo:D_FF + lo + width])
        a = (gate * jax.nn.sigmoid(gate) * up).astype(BF16)
        acc = acc + _dot(a, wd_ref[lo:lo + width, :])
    return x + 0.5 * acc


def _ffn_kernel(x_ref, g_ref, wup_ref, wd_ref, o_ref):
    o_ref[...] = _ffn_body(x_ref, g_ref, wup_ref, wd_ref)


def _ffn_final_kernel(x_ref, g_ref, wup_ref, wd_ref, fg_ref, o_ref):
    o_ref[...] = _rms(_ffn_body(x_ref, g_ref, wup_ref, wd_ref), fg_ref[...])


def _ffn(x2, g, wup, wd, layer, final_g=None):
    n = x2.shape[0]
    row = pl.BlockSpec((ROW_TILE, D_MODEL), lambda i: (i, 0))
    in_specs = [row, _const_spec((1, D_MODEL)), _layer_spec(wup, layer), _layer_spec(wd, layer)]
    args = [x2, g, wup, wd]
    body = _ffn_kernel
    if final_g is not None:
        in_specs.append(_const_spec((1, D_MODEL)))
        args.append(final_g)
        body = _ffn_final_kernel
    return pl.pallas_call(
        body, grid=(n // ROW_TILE,), in_specs=in_specs, out_specs=row,
        out_shape=jax.ShapeDtypeStruct(x2.shape, F32),
        compiler_params=_params(("parallel",)), name="ffn")(*args)


def _inproj_kernel(x_ref, g_ref, w_ref, pa_ref, pb_ref, pc_ref, pd_ref):
    h = _rms(x_ref[...], g_ref[...]).astype(BF16)
    off = 0
    for ref, width in ((pa_ref, PA_W), (pb_ref, PB_W), (pc_ref, PC_W), (pd_ref, PD_W)):
        ref[...] = _dot(h, w_ref[:, off:off + width])
        off += width


def _inproj(x2, g, w, layer):
    n = x2.shape[0]
    widths = (PA_W, PB_W, PC_W, PD_W)
    return pl.pallas_call(
        _inproj_kernel, grid=(n // ROW_TILE,),
        in_specs=[pl.BlockSpec((ROW_TILE, D_MODEL), lambda i: (i, 0)),
                  _const_spec((1, D_MODEL)), _layer_spec(w, layer)],
        out_specs=[pl.BlockSpec((ROW_TILE, wd), lambda i: (i, 0)) for wd in widths],
        out_shape=[jax.ShapeDtypeStruct((n, wd), F32) for wd in widths],
        compiler_params=_params(("parallel",)), name="inproj")(x2, g, w)


def _memkv_kernel(m_ref, g_ref, w_ref, kv_ref):
    h = _rms(m_ref[...], g_ref[...]).astype(BF16)
    kv_ref[...] = _dot(h, w_ref[...]).astype(BF16)


def _memkv(mem2, g, w, layer):
    n = mem2.shape[0]
    return pl.pallas_call(
        _memkv_kernel, grid=(n // ROW_TILE,),
        in_specs=[pl.BlockSpec((ROW_TILE, D_MODEL), lambda i: (i, 0)),
                  _const_spec((1, D_MODEL)), _layer_spec(w, layer)],
        out_specs=pl.BlockSpec((ROW_TILE, 2 * D_MODEL), lambda i: (i, 0)),
        out_shape=jax.ShapeDtypeStruct((n, 2 * D_MODEL), BF16),
        compiler_params=_params(("parallel",)), name="memkv")(mem2, g, w)


def _cross_kernel(x_ref, oa_ref, ob_ref, oc_ref, od_ref, wout_ref, g_ref, wq_ref, kv_ref, wo_ref, o_ref):
    x = x_ref[...]
    for j, ref in enumerate((oa_ref, ob_ref, oc_ref, od_ref)):
        x = x + _dot(ref[...], wout_ref[j * GROUP_WIDTH:(j + 1) * GROUP_WIDTH, :])
    h = _rms(x, g_ref[...]).astype(BF16)
    q = _dot(h, wq_ref[...]).astype(BF16)
    heads = []
    for hd in range(HEADS):
        lo = hd * X_HD
        s = _dot_t(q[:, lo:lo + X_HD], kv_ref[:, lo:lo + X_HD]) * (X_HD ** -0.5)
        s = s - jnp.max(s, axis=-1, keepdims=True)
        p = jnp.exp(s)
        p = p / jnp.sum(p, axis=-1, keepdims=True)
        heads.append(_dot(p.astype(BF16), kv_ref[:, D_MODEL + lo:D_MODEL + lo + X_HD]).astype(BF16))
    o = jnp.concatenate(heads, axis=-1)
    o_ref[...] = x + _dot(o, wo_ref[...])


def _cross(x2, outs, w_out, g, wq, kv, wo, layer, seq, mem_len):
    n = x2.shape[0]
    tiles_per_seq = seq // ROW_TILE
    row = pl.BlockSpec((ROW_TILE, D_MODEL), lambda i: (i, 0))
    grp = pl.BlockSpec((ROW_TILE, GROUP_WIDTH), lambda i: (i, 0))
    return pl.pallas_call(
        _cross_kernel, grid=(n // ROW_TILE,),
        in_specs=[row, grp, grp, grp, grp, _layer_spec(w_out, layer),
                  _const_spec((1, D_MODEL)), _layer_spec(wq, layer),
                  pl.BlockSpec((mem_len, 2 * D_MODEL), lambda i: (i // tiles_per_seq, 0)),
                  _layer_spec(wo, layer)],
        out_specs=row, out_shape=jax.ShapeDtypeStruct(x2.shape, F32),
        compiler_params=_params(("parallel",)), name="cross")(x2, *outs, w_out, g, wq, kv, wo)


def _gla_constants(dk):
    c, h = GLA_CHUNK, HEADS
    w, wv = h * dk, h * HEAD_DV
    r = np.arange(c)
    blocks = [(r[:, None] >= r[None, :])]
    masks = [(r[:, None] == r[None, :])]
    m = 1
    while m < c:
        ref = (r // (2 * m)) * (2 * m) + m - 1
        upper = (r % (2 * m)) >= m
        rp = r[None, :]
        rng_up = (rp > ref[:, None]) & (rp <= r[:, None])
        rng_lo = (rp > r[:, None]) & (rp <= ref[:, None])
        blocks.append(np.where(upper[:, None], rng_up, rng_lo))
        same = (r[:, None] // (2 * m)) == (r[None, :] // (2 * m))
        masks.append(same & upper[:, None] & (~upper)[None, :])
        m *= 2
    nmat = np.concatenate(blocks, axis=0).astype(np.float32)
    lvl = np.stack([np.tile(mk, (1, h)) for mk in masks]).astype(np.float32)
    rows = np.arange(h * c)[:, None] // c
    hm_k = (rows == (np.arange(w)[None, :] // dk)).astype(np.float32)
    hm_v = (rows == (np.arange(wv)[None, :] // HEAD_DV)).astype(np.float32)
    bm_t = ((np.arange(wv)[:, None] // HEAD_DV) == (np.arange(w)[None, :] // dk)).astype(np.float32)
    ones_blk = ((np.arange(wv)[:, None] // HEAD_DV) == (np.arange(wv)[None, :] // HEAD_DV))
    return (jnp.asarray(nmat, BF16), jnp.asarray(lvl, F32), jnp.asarray(hm_k, BF16),
            jnp.asarray(hm_v, BF16), jnp.asarray(bm_t, F32), jnp.asarray(ones_blk.astype(np.float32), BF16))


def _gla_recurrence(seq, q_ref, k_ref, v_ref, g_ref, o_ref, st_ref,
                    nmat_ref, lvl_ref, hmk_ref, hmv_ref, bmt_ref):
    c, h = GLA_CHUNK, HEADS
    n_lvl = lvl_ref.shape[0]
    st_ref[...] = jnp.zeros_like(st_ref)

    def step(ci, carry):
        r0 = pl.multiple_of(ci * c, c)
        q = q_ref[pl.ds(r0, c), :]
        k = k_ref[pl.ds(r0, c), :]
        v = v_ref[pl.ds(r0, c), :]
        g = g_ref[pl.ds(r0, c), :]
        g_hi = g.astype(BF16)
        g_mid = (g - g_hi.astype(F32)).astype(BF16)
        cum = nmat_ref[0:c, :]
        b = _dot(cum, g_hi) + _dot(cum, g_mid)
        lvl_rows = nmat_ref[c:, :]
        sums = _dot(lvl_rows, g_hi)
        hmk = hmk_ref[...]
        acc = jnp.zeros((c, h * c), F32)
        for li in range(n_lvl):
            if li == 0:
                qt, kt = q, k
            else:
                e = jnp.exp(sums[(li - 1) * c:li * c])
                qt, kt = q * e, k * e
            kst = jnp.concatenate([kt.astype(BF16)] * h, axis=0) * hmk
            acc = acc + _dot_t(qt.astype(BF16), kst) * lvl_ref[li]
        vb = v.astype(BF16)
        vst = jnp.concatenate([vb] * h, axis=0) * hmv_ref[...]
        o = _dot(acc.astype(BF16), vst)
        st = st_ref[...]
        o = o + _dot_t((q * jnp.exp(b)).astype(BF16), st.astype(BF16))
        b_last = b[c - 1:c, :]
        kh = (k * jnp.exp(b_last - b)).astype(BF16)
        st_ref[...] = st * jnp.exp(b_last) + _tdot(vb, kh) * bmt_ref[...]
        o_ref[pl.ds(r0, c), :] = o
        return carry

    lax.fori_loop(0, seq // c, step, 0, unroll=GLA_UNROLL)


def _head_rms_gate(o, gain, gate, ones_blk):
    sq = o * o
    hi = sq.astype(BF16)
    lo = (sq - hi.astype(F32)).astype(BF16)
    ms = (_dot(hi, ones_blk) + _dot(lo, ones_blk)) * (1.0 / HEAD_DV)
    return o * lax.rsqrt(ms + EPS) * gain * (gate * jax.nn.sigmoid(gate))


def _hgrn_kernel(p_ref, lb_ref, gain_ref, nmat_ref, lvl_ref, hmk_ref, hmv_ref, bmt_ref, ones_ref,
                 o_ref, q_s, k_s, g_s, o_s, st_s):
    seq, gw = o_ref.shape
    z = p_ref[:, gw:2 * gw]
    lb = lb_ref[...]
    lbf = jnp.maximum(lb, HG_LB_FLOOR)
    e = jnp.exp(-jnp.abs(z))
    inv = 1.0 / (1.0 + e)
    sig = jnp.where(z >= 0, inv, e * inv)
    nsig = jnp.where(z >= 0, e * inv, inv)
    f = lbf + (1.0 - lb) * sig
    g_s[...] = jnp.log(f)
    k_s[...] = (1.0 - lb) * nsig - (lbf - lb)
    q_s[...] = p_ref[:, 0:gw]
    _gla_recurrence(seq, q_s, k_s, p_ref.at[:, 2 * gw:3 * gw], g_s, o_s, st_s,
                    nmat_ref, lvl_ref, hmk_ref, hmv_ref, bmt_ref)
    o_ref[...] = _head_rms_gate(o_s[...], gain_ref[...], p_ref[:, 3 * gw:4 * gw],
                                ones_ref[...]).astype(o_ref.dtype)


def _gla_kernel(p_ref, gw_ref, gb_ref, gain_ref, nmat_ref, lvl_ref, hmk_ref, hmv_ref, bmt_ref,
                ones_ref, o_ref, q_s, g_s, o_s, st_s):
    seq, gw = o_ref.shape
    wk = HEADS * GLA_DK
    lr = p_ref[:, 2 * wk + 2 * gw:2 * wk + 2 * gw + LANES]
    lr_hi = lr.astype(BF16)
    lr_lo = (lr - lr_hi.astype(F32)).astype(BF16)
    w = gw_ref[...]
    w_hi = w.astype(BF16)
    w_lo = (w - w_hi.astype(F32)).astype(BF16)
    y = _dot(lr_hi, w_hi) + _dot(lr_hi, w_lo) + _dot(lr_lo, w_hi) + gb_ref[...]
    g_s[...] = (jnp.minimum(y, 0.0) - jnp.log(1.0 + jnp.exp(-jnp.abs(y)))) * (1.0 / GLA_TAU)
    q_s[...] = p_ref[:, 0:wk] * (GLA_DK ** -0.5)
    _gla_recurrence(seq, q_s, p_ref.at[:, wk:2 * wk], p_ref.at[:, 2 * wk:2 * wk + gw], g_s, o_s, st_s,
                    nmat_ref, lvl_ref, hmk_ref, hmv_ref, bmt_ref)
    o_ref[...] = _head_rms_gate(o_s[...], gain_ref[...], p_ref[:, 2 * wk + gw:2 * wk + 2 * gw],
                                ones_ref[...]).astype(o_ref.dtype)


def _seq_spec(seq, width):
    return pl.BlockSpec((seq, width), lambda b: (b, 0))


def _hgrn(pa, lb, gain, batch, seq):
    consts = _gla_constants(HG_DK)
    w = HEADS * HG_DK
    return pl.pallas_call(
        _hgrn_kernel, grid=(batch,),
        in_specs=[_seq_spec(seq, PA_W), _const_spec((1, w)), _const_spec((1, GROUP_WIDTH))]
                 + [_const_spec(cst.shape) for cst in consts],
        out_specs=_seq_spec(seq, GROUP_WIDTH),
        out_shape=jax.ShapeDtypeStruct((batch * seq, GROUP_WIDTH), BF16),
        scratch_shapes=[pltpu.VMEM((seq, w), F32), pltpu.VMEM((seq, w), F32), pltpu.VMEM((seq, w), F32),
                        pltpu.VMEM((seq, GROUP_WIDTH), F32), pltpu.VMEM((GROUP_WIDTH, w), F32)],
        compiler_params=_params(("parallel",)), name="hgrn")(pa, lb, gain, *consts)


def _gla(pb, gate_w, gate_b, gain, batch, seq):
    consts = _gla_constants(GLA_DK)
    w = HEADS * GLA_DK
    return pl.pallas_call(
        _gla_kernel, grid=(batch,),
        in_specs=[_seq_spec(seq, PB_W), _const_spec((LANES, w)), _const_spec((1, w)),
                  _const_spec((1, GROUP_WIDTH))] + [_const_spec(cst.shape) for cst in consts],
        out_specs=_seq_spec(seq, GROUP_WIDTH),
        out_shape=jax.ShapeDtypeStruct((batch * seq, GROUP_WIDTH), BF16),
        scratch_shapes=[pltpu.VMEM((seq, w), F32), pltpu.VMEM((seq, w), F32),
                        pltpu.VMEM((seq, GROUP_WIDTH), F32), pltpu.VMEM((GROUP_WIDTH, w), F32)],
        compiler_params=_params(("parallel",)), name="gla")(pb, gate_w, gate_b, gain, *consts)


def _dilated_kernel(p_ref, cos_ref, sin_ref, o_ref, q_s, k_s, v_s, op_s, lse_s):
    seq, gw = o_ref.shape
    n_half = gw // LANES
    heads_per_half = LANES // DA_HD
    lane = lax.broadcasted_iota(jnp.int32, (1, LANES), 1)
    cos, sin = cos_ref[...], sin_ref[...]

    src = lax.broadcasted_iota(jnp.int32, (LANES, LANES), 0)
    dst = lax.broadcasted_iota(jnp.int32, (LANES, LANES), 1)
    dst_in_head = dst & (DA_HD - 1)
    half = DA_ROT // 2
    perm = (((dst_in_head < half) & (src == dst + half))
            | ((dst_in_head >= half) & (dst_in_head < DA_ROT) & (src == dst - half)))
    perm = jnp.where(perm, 1.0, 0.0).astype(BF16)

    def rope(t):
        t_hi = t.astype(BF16)
        t_lo = (t - t_hi.astype(F32)).astype(BF16)
        partner = _dot(t_hi, perm) + _dot(t_lo, perm)
        return t * cos + partner * sin

    for hf in range(n_half):
        lo = hf * LANES
        q_s[hf] = rope(p_ref[:, lo:lo + LANES]) * (DA_HD ** -0.5 * LOG2_E)
        k_s[hf] = rope(p_ref[:, gw + lo:gw + lo + LANES])
        v_s[hf] = p_ref[:, 2 * gw + lo:2 * gw + lo + LANES]

    nq = DA_STEPS
    qi = lax.broadcasted_iota(jnp.int32, (nq, 2 * nq), 0)
    kj = lax.broadcasted_iota(jnp.int32, (nq, 2 * nq), 1)
    cur_ok = (kj >= nq) & (kj - nq <= qi)
    head_of_lane = lane >> int(math.log2(DA_HD))

    for pi, (window, dil) in enumerate(DA_PATTERNS):
        n_blk = seq // (dil * nq)

        def block(i, carry, pi=pi, dil=dil, n_blk=n_blk):
            r = i // n_blk
            n = i % n_blk
            cur0 = r + n * (nq * dil)
            prev0 = r + jnp.maximum(n - 1, 0) * (nq * dil)

            def rows(start):
                if dil == 1:
                    return pl.ds(pl.multiple_of(start, nq), nq)
                return pl.ds(start, nq, stride=dil)

            first_row = qi + jnp.where(n > 0, 0, nq)
            valid = cur_ok | ((kj < nq) & (kj >= first_row))
            for hf in range(n_half):
                q = q_s[hf, rows(cur0), :]
                kwin = jnp.concatenate([k_s[hf, rows(prev0), :], k_s[hf, rows(cur0), :]], axis=0).astype(BF16)
                vwin = jnp.concatenate([v_s[hf, rows(prev0), :], v_s[hf, rows(cur0), :]], axis=0).astype(BF16)
                o_acc = jnp.zeros((nq, LANES), F32)
                lse_acc = jnp.zeros((nq, LANES), F32)
                for hd in range(heads_per_half):
                    hm = head_of_lane == hd
                    s = _dot_t(jnp.where(hm, q, 0.0).astype(BF16), kwin)
                    s = jnp.where(valid, s, MASK_VALUE)
                    m = jnp.max(s, axis=-1, keepdims=True)
                    p = jnp.exp2(s - m)
                    l = jnp.sum(p, axis=-1, keepdims=True)
                    oh = _dot(p.astype(BF16), vwin) / l
                    o_acc = jnp.where(hm, oh, o_acc)
                    lse_acc = jnp.where(hm, m * (1.0 / LOG2_E) + jnp.log(l), lse_acc)
                op_s[pi, hf, rows(cur0), :] = o_acc
                lse_s[pi, hf, rows(cur0), :] = lse_acc
            return carry

        lax.fori_loop(0, dil * n_blk, block, 0, unroll=DA_UNROLL)

    for hf in range(n_half):
        lses = [lse_s[pi, hf] for pi in range(len(DA_PATTERNS))]
        mx = functools.reduce(jnp.maximum, lses)
        ws = [jnp.exp(l - mx) for l in lses]
        num = sum(wt * op_s[pi, hf] for pi, wt in enumerate(ws))
        o_ref[:, hf * LANES:(hf + 1) * LANES] = (num / sum(ws)).astype(o_ref.dtype)


def _dilated(pc, cos_t, sin_t, batch, seq):
    gw = GROUP_WIDTH
    n_pat = len(DA_PATTERNS)
    n_half = gw // LANES
    return pl.pallas_call(
        _dilated_kernel, grid=(batch,),
        in_specs=[_seq_spec(seq, PC_W), _seq_spec(seq, LANES), _seq_spec(seq, LANES)],
        out_specs=_seq_spec(seq, gw),
        out_shape=jax.ShapeDtypeStruct((batch * seq, gw), BF16),
        scratch_shapes=[pltpu.VMEM((n_half, seq, LANES), F32)] * 3
                       + [pltpu.VMEM((n_pat, n_half, seq, LANES), F32)] * 2,
        compiler_params=_params(("parallel",)), name="dilated")(pc, cos_t, sin_t)


def _conv_kernel(p_ref, w_ref, b_ref, lg_ref, lb_ref, o_ref, u_s):
    seq, gw = o_ref.shape
    a = p_ref[:, 0:gw]
    gate = p_ref[:, gw:2 * gw]
    u_s[0:CONV_PAD, :] = jnp.zeros((CONV_PAD, gw), F32)
    u_s[CONV_PAD:CONV_PAD + seq, :] = a * jax.nn.sigmoid(gate)
    u_s[CONV_PAD + seq:CONV_PAD + seq + SUBLANES, :] = jnp.zeros((SUBLANES, gw), F32)
    w = w_ref[...]
    first = CONV_PAD - (CONV_K - 1)
    ext = CONV_TILE + SUBLANES

    def tile(ti, carry):
        t0 = pl.multiple_of(ti * CONV_TILE, CONV_TILE)
        y = jnp.zeros((CONV_TILE, gw), F32) + b_ref[...]
        for s in range(SUBLANES):
            z = None
            for a in range(s, first + CONV_K, SUBLANES):
                if a >= first:
                    term = u_s[pl.ds(pl.multiple_of(t0 + (a - s), SUBLANES), ext), :] * w[a - first:a - first + 1, :]
                    z = term if z is None else z + term
            y = y + z[s:s + CONV_TILE, :]
        mu = jnp.mean(y, axis=-1, keepdims=True)
        d = y - mu
        var = jnp.mean(d * d, axis=-1, keepdims=True)
        yn = d * lax.rsqrt(var + EPS) * lg_ref[...] + lb_ref[...]
        o_ref[pl.ds(t0, CONV_TILE), :] = (yn * jax.nn.sigmoid(yn)).astype(o_ref.dtype)
        return carry

    lax.fori_loop(0, seq // CONV_TILE, tile, 0, unroll=CONV_UNROLL)


def _conv(pd, w, b, ln_g, ln_b, batch, seq):
    gw = GROUP_WIDTH
    vec = _const_spec((1, gw))
    return pl.pallas_call(
        _conv_kernel, grid=(batch,),
        in_specs=[_seq_spec(seq, PD_W), _const_spec(w.shape), vec, vec, vec],
        out_specs=_seq_spec(seq, gw),
        out_shape=jax.ShapeDtypeStruct((batch * seq, gw), BF16),
        scratch_shapes=[pltpu.VMEM((CONV_PAD + seq + SUBLANES, gw), F32)],
        compiler_params=_params(("parallel",)), name="conv")(pd, w, b, ln_g, ln_b)


def _rope_tables(positions):
    half = DA_ROT // 2
    inv_freq = jnp.power(jnp.float32(ROPE_THETA), -jnp.arange(0, DA_ROT, 2, dtype=F32) / DA_ROT)
    ang = positions.astype(F32)[..., None] * inv_freq
    cos, sin = jnp.cos(ang), jnp.sin(ang)
    rest = DA_HD - DA_ROT
    ones = jnp.ones(cos.shape[:-1] + (rest,), F32)
    cos_h = jnp.concatenate([cos, cos, ones], axis=-1)
    sin_h = jnp.concatenate([-sin, sin, 0.0 * ones], axis=-1)
    b, s = positions.shape
    tile = lambda t: jnp.tile(t, (1, 1, LANES // DA_HD)).reshape(b * s, LANES)
    return tile(cos_h), tile(sin_h)


def _relayout_w_in(w_in):
    a_end = PA_W
    wk = HEADS * GLA_DK
    b_qkv = w_in[..., a_end:a_end + 2 * wk + GROUP_WIDTH]
    lr0 = a_end + 2 * wk + GROUP_WIDTH
    b_lr = w_in[..., lr0:lr0 + GLA_RANK]
    b_r = w_in[..., lr0 + GLA_RANK:lr0 + GLA_RANK + GROUP_WIDTH]
    c0 = lr0 + GLA_RANK + GROUP_WIDTH
    rest = w_in[..., c0:]
    pad = jnp.zeros(w_in.shape[:-1] + (LANES - GLA_RANK,), w_in.dtype)
    return jnp.concatenate([w_in[..., :a_end], b_qkv, b_r, b_lr, pad, rest], axis=-1).astype(BF16)


def kernel(x, mem, positions, hgrn_lb_logits, ffn1_norm, ffn1_w_up, ffn1_w_down, mix_norm, w_in, hgrn_out_norm, gla_gate_w, gla_gate_b, gla_out_norm, conv_w, conv_b, conv_ln_g, conv_ln_b, w_out, cross_norm, mem_norm, cross_wq, cross_wkv, cross_wo, ffn2_norm, ffn2_w_up, ffn2_w_down, final_norm):
    batch, seq, d = x.shape
    mem_len = mem.shape[1]
    depth = w_in.shape[0]
    assert d == D_MODEL and seq % (DA_PATTERNS[-1][1] * DA_STEPS) == 0
    assert seq % ROW_TILE == 0 and (batch * mem_len) % ROW_TILE == 0

    cos_t, sin_t = _rope_tables(positions)
    p_lb = jax.nn.softmax(hgrn_lb_logits.astype(F32), axis=0)
    lower_bounds = jnp.cumsum(p_lb, axis=0) - p_lb[0:1]

    row = lambda v: v.reshape(1, -1).astype(F32)
    x2 = x.reshape(batch * seq, d)
    mem2 = mem.reshape(batch * mem_len, d)
    up1, down1 = ffn1_w_up.astype(BF16), ffn1_w_down.astype(BF16)
    up2, down2 = ffn2_w_up.astype(BF16), ffn2_w_down.astype(BF16)
    w_in_b, w_out_b = _relayout_w_in(w_in), w_out.astype(BF16)
    wq_b, wkv_b, wo_b = cross_wq.astype(BF16), cross_wkv.astype(BF16), cross_wo.astype(BF16)
    for l in range(depth):
        x2 = _ffn(x2, row(ffn1_norm[l]), up1, down1, l)

        pa, pb, pc, pd = _inproj(x2, row(mix_norm[l]), w_in_b, l)
        gate_w = jnp.zeros((LANES, HEADS * GLA_DK), F32).at[:GLA_RANK].set(gla_gate_w[l])
        o_a = _hgrn(pa, row(lower_bounds[l]), row(hgrn_out_norm[l]), batch, seq)
        o_b = _gla(pb, gate_w, row(gla_gate_b[l]), row(gla_out_norm[l]), batch, seq)
        o_c = _dilated(pc, cos_t, sin_t, batch, seq)
        o_d = _conv(pd, conv_w[l], row(conv_b[l]), row(conv_ln_g[l]), row(conv_ln_b[l]), batch, seq)
        kv = _memkv(mem2, row(mem_norm[l]), wkv_b, l)
        x2 = _cross(x2, (o_a, o_b, o_c, o_d), w_out_b, row(cross_norm[l]), wq_b, kv, wo_b, l, seq, mem_len)

        last = l == depth - 1
        x2 = _ffn(x2, row(ffn2_norm[l]), up2, down2, l, final_g=row(final_norm) if last else None)
    return x2.reshape(batch, seq, d)
```
